```python
import jax, jax.numpy as jnp
from jax import lax
import numpy as np

D_MODEL = 1024
BATCH = 2
SEQ = 8192
DEPTH = 2
DEC_BATCH = 16
DEC_SEQ = 2048
PAST_LEN = 128

N_MIXERS = 2
CONV_WIDTH = 3
MLSTM_HEADS = 8
QK_DIM = D_MODEL // (2 * MLSTM_HEADS)
V_DIM = D_MODEL // MLSTM_HEADS
QK_W = MLSTM_HEADS * QK_DIM
MLSTM_PROJ = 2 * QK_W + 2 * D_MODEL + 4 * MLSTM_HEADS
CHUNK = 128
D_FF = -(-8 * D_MODEL // (3 * 256)) * 256
EPS = 1e-6

kernel_name = "hybrid_conv_mlstm_bidir_encoder"


def rms_norm(x, g):
    xf = x.astype(jnp.float32)
    y = xf * lax.rsqrt(jnp.mean(xf * xf, axis=-1, keepdims=True) + EPS)
    return (y * g.astype(jnp.float32)).astype(x.dtype)


def short_conv_mixer(x, w_in, conv_w, w_out):
    S = x.shape[1]
    b, c, v = jnp.split(x @ w_in, 3, axis=-1)
    u = c * v
    pad = CONV_WIDTH // 2
    up = jnp.pad(u, ((0, 0), (pad, pad), (0, 0)))
    conv = sum(up[:, j:j + S] * conv_w[j] for j in range(CONV_WIDTH))
    return (b * conv) @ w_out


def mlstm_chunkwise(q, k, v, log_i, log_f):
    Bn, H, S, dk = q.shape
    dv = v.shape[-1]
    nc = S // CHUNK

    def to_chunks(a):
        return jnp.moveaxis(a.reshape((Bn, H, nc, CHUNK) + a.shape[3:]), 2, 0)

    xs = tuple(map(to_chunks, (q, k, v, log_i, log_f)))
    lower = jnp.tril(jnp.ones((CHUNK, CHUNK), dtype=bool))

    def step(carry, xs_c):
        C, n, m = carry
        qb, kb, vb, ib, fb = xs_c
        bcum = jnp.cumsum(fb, axis=-1)
        dmat = bcum[..., :, None] - bcum[..., None, :] + ib[..., None, :]
        dmat = jnp.where(lower, dmat, -jnp.inf)
        m_inter = bcum + m[..., None]
        m_t = jnp.maximum(m_inter, jnp.max(dmat, axis=-1))
        w_intra = jnp.exp(dmat - m_t[..., None])
        w_inter = jnp.exp(m_inter - m_t)
        scores = jnp.einsum('bhtd,bhsd->bhts', qb, kb) * w_intra
        num = (jnp.einsum('bhts,bhsv->bhtv', scores, vb)
               + w_inter[..., None] * jnp.einsum('bhtd,bhdv->bhtv', qb, C))
        den = jnp.sum(scores, axis=-1) + w_inter * jnp.einsum('bhtd,bhd->bht', qb, n)
        h = num / jnp.maximum(jnp.abs(den), jnp.exp(-m_t))[..., None]
        b_last = bcum[..., -1]
        d_last = b_last[..., None] - bcum + ib
        m_new = jnp.maximum(b_last + m, jnp.max(d_last, axis=-1))
        w_s = jnp.exp(d_last - m_new[..., None])
        decay = jnp.exp(b_last + m - m_new)
        C_new = decay[..., None, None] * C + jnp.einsum('bhs,bhsd,bhsv->bhdv', w_s, kb, vb)
        n_new = decay[..., None] * n + jnp.einsum('bhs,bhsd->bhd', w_s, kb)
        return (C_new, n_new, m_new), h

    init = (jnp.zeros((Bn, H, dk, dv), jnp.float32),
            jnp.zeros((Bn, H, dk), jnp.float32),
            jnp.zeros((Bn, H), jnp.float32))
    _, hs = lax.scan(step, init, xs)
    return jnp.moveaxis(hs, 0, 2).reshape(Bn, H, S, dv)


def mlstm_mixer(x, w_in, b_gate, norm_g, w_out):
    Bn, S, _ = x.shape
    H = MLSTM_HEADS
    proj = x @ w_in
    q, k, v, o, gates = jnp.split(
        proj, [QK_W, 2 * QK_W, 2 * QK_W + D_MODEL, 2 * QK_W + 2 * D_MODEL], axis=-1)

    def heads(a, d):
        return a.reshape(Bn, S, H, d).transpose(0, 2, 1, 3).astype(jnp.float32)

    qh = heads(q, QK_DIM) * (QK_DIM ** -0.5)
    kh = heads(k, QK_DIM)
    vh = heads(v, V_DIM)
    g = (gates.astype(jnp.float32) + b_gate.astype(jnp.float32)).transpose(0, 2, 1)
    i_f, f_f, i_b, f_b = jnp.split(g, 4, axis=1)
    h_fwd = mlstm_chunkwise(qh, kh, vh, i_f, jax.nn.log_sigmoid(f_f))
    flip = lambda a: jnp.flip(a, axis=2)
    h_bwd = flip(mlstm_chunkwise(flip(qh), flip(kh), flip(vh), flip(i_b),
                                 flip(jax.nn.log_sigmoid(f_b))))
    h = h_fwd + h_bwd
    h = h * lax.rsqrt(jnp.mean(h * h, axis=-1, keepdims=True) + EPS)
    h = h.transpose(0, 2, 1, 3).reshape(Bn, S, D_MODEL) * norm_g.astype(jnp.float32)
    h = (jax.nn.sigmoid(o.astype(jnp.float32)) * h).astype(x.dtype)
    return h @ w_out


def swiglu(x, w_in, w_out):
    gate, up = jnp.split(x @ w_in, 2, axis=-1)
    return (jax.nn.silu(gate) * up) @ w_out


def trunk(x, norms, conv_w_in, conv_w, conv_w_out,
          mlstm_w_in, mlstm_b_gate, mlstm_norm, mlstm_w_out, ffn_w_in, ffn_w_out):
    for i in range(DEPTH):
        j = i // N_MIXERS
        h = rms_norm(x, norms[i, 0])
        if i % N_MIXERS == 0:
            mix = short_conv_mixer(h, conv_w_in[j], conv_w[j], conv_w_out[j])
        else:
            mix = mlstm_mixer(h, mlstm_w_in[j], mlstm_b_gate[j], mlstm_norm[j], mlstm_w_out[j])
        x = x + rms_norm(mix, norms[i, 1])
        h = rms_norm(x, norms[i, 2])
        x = x + rms_norm(swiglu(h, ffn_w_in[i], ffn_w_out[i]), norms[i, 3])
    return x


def setup_inputs(seed: int = 0) -> dict:
    key = jax.random.key(seed)
    ks = jax.random.split(key, 16)
    n_conv = (DEPTH + N_MIXERS - 1) // N_MIXERS
    n_ml = DEPTH // N_MIXERS
    H = MLSTM_HEADS
    nrm = lambda k, s, scale: jax.random.normal(k, s, jnp.float32) * scale
    gb = jax.random.normal(ks[7], (n_ml, 4 * H), jnp.float32)
    gate_offset = jnp.concatenate([jnp.zeros((H,)), 3.0 * jnp.ones((H,)),
                                   jnp.zeros((H,)), 3.0 * jnp.ones((H,))]).astype(jnp.float32)
    return {
        "x_prompt": nrm(ks[0], (BATCH, SEQ, D_MODEL), 1.0),
        "x_sample": nrm(ks[1], (DEC_BATCH, DEC_SEQ, D_MODEL), 1.0),
        "norms": 1.0 + nrm(ks[2], (DEPTH, 4, D_MODEL), 0.05),
        "conv_w_in": nrm(ks[3], (n_conv, D_MODEL, 3 * D_MODEL), D_MODEL ** -0.5),
        "conv_w": nrm(ks[4], (n_conv, CONV_WIDTH, D_MODEL), CONV_WIDTH ** -0.5),
        "conv_w_out": nrm(ks[5], (n_conv, D_MODEL, D_MODEL), D_MODEL ** -0.5),
        "mlstm_w_in": nrm(ks[6], (n_ml, D_MODEL, MLSTM_PROJ), D_MODEL ** -0.5),
        "mlstm_b_gate": gate_offset + 0.3 * gb,
        "mlstm_norm": 1.0 + nrm(ks[8], (n_ml, D_MODEL), 0.05),
        "mlstm_w_out": nrm(ks[9], (n_ml, D_MODEL, D_MODEL), D_MODEL ** -0.5),
        "ffn_w_in": nrm(ks[10], (DEPTH, D_MODEL, 2 * D_FF), D_MODEL ** -0.5),
        "ffn_w_out": nrm(ks[11], (DEPTH, D_FF, D_MODEL), D_FF ** -0.5),
    }


def reference(x_prompt, x_sample, norms, conv_w_in, conv_w, conv_w_out,
              mlstm_w_in, mlstm_b_gate, mlstm_norm, mlstm_w_out, ffn_w_in, ffn_w_out):
    y_prompt = trunk(x_prompt, norms, conv_w_in, conv_w, conv_w_out,
                     mlstm_w_in, mlstm_b_gate, mlstm_norm, mlstm_w_out, ffn_w_in, ffn_w_out)
    y_sample = trunk(x_sample, norms, conv_w_in, conv_w, conv_w_out,
                     mlstm_w_in, mlstm_b_gate, mlstm_norm, mlstm_w_out, ffn_w_in, ffn_w_out)
    return (y_prompt, y_sample)
```

```python
import functools

import jax
import jax.numpy as jnp
from jax import lax
from jax.experimental import pallas as pl
from jax.experimental.pallas import tpu as pltpu

F32 = jnp.float32
BF16 = jnp.bfloat16

D_MODEL = 1024
HEADS = 8
QK_DIM = 64
V_DIM = 128
QK_W = HEADS * QK_DIM
CHUNK = 128
D_FF = 2816
EPS = 1e-6
CONV_HALO = 16
UNITS = 2 * HEADS

VMEM_LIMIT_BYTES = 56 * 1024 * 1024

TOKEN_TILE = 512
CHUNKS_PER_STEP = 4


def _rms(x, g):
    ms = jnp.mean(x * x, axis=-1, keepdims=True)
    return x * lax.rsqrt(ms + EPS) * g


def _dot(a, b):
    return jnp.dot(a, b, preferred_element_type=F32)


def _const_spec(shape):
    nd = len(shape)
    return pl.BlockSpec(shape, lambda *_: (0,) * nd, pipeline_mode=pl.Buffered(1))


def _params(n_axes):
    return pltpu.CompilerParams(
        dimension_semantics=("arbitrary",) * n_axes,
        vmem_limit_bytes=VMEM_LIMIT_BYTES,
    )


def _conv_kernel(x_ref, xp_ref, xn_ref, n_ref, win_ref, cw_ref, wout_ref, o_ref, h_scr, u_scr, *, tm, nt):
    i = pl.program_id(1)
    d = D_MODEL
    hl = CONV_HALO
    g_in = n_ref[0:1, :]
    x = x_ref[0]
    h_scr[0:hl, :] = _rms(xp_ref[0], g_in).astype(BF16)
    h_scr[hl:hl + tm, :] = _rms(x, g_in).astype(BF16)
    h_scr[hl + tm:, :] = _rms(xn_ref[0], g_in).astype(BF16)
    cv = _dot(h_scr[...], win_ref[:, d:3 * d])
    u_scr[...] = cv[:, :d] * cv[:, d:]

    @pl.when(i == 0)
    def _():
        u_scr[0:hl, :] = jnp.zeros((hl, d), F32)

    @pl.when(i == nt - 1)
    def _():
        u_scr[hl + tm:, :] = jnp.zeros((hl, d), F32)

    conv = (u_scr[hl - 1:hl - 1 + tm, :] * cw_ref[0:1, :]
            + u_scr[hl:hl + tm, :] * cw_ref[1:2, :]
            + u_scr[hl + 1:hl + 1 + tm, :] * cw_ref[2:3, :])
    b = _dot(h_scr[hl:hl + tm, :], win_ref[:, 0:d])
    mix = _dot((b * conv).astype(BF16), wout_ref[...])
    o_ref[0] = x + _rms(mix, n_ref[1:2, :])


def _conv_mixer(x, norms2, w_in, conv_w, w_out):
    bsz, s, d = x.shape
    tm = min(TOKEN_TILE, s)
    nt = s // tm
    hb = tm // CONV_HALO
    last_hb = s // CONV_HALO - 1
    kern = functools.partial(_conv_kernel, tm=tm, nt=nt)
    return pl.pallas_call(
        kern,
        grid=(bsz, nt),
        in_specs=[
            pl.BlockSpec((1, tm, d), lambda b, i: (b, i, 0)),
            pl.BlockSpec((1, CONV_HALO, d), lambda b, i: (b, jnp.maximum(i * hb - 1, 0), 0)),
            pl.BlockSpec((1, CONV_HALO, d), lambda b, i: (b, jnp.minimum((i + 1) * hb, last_hb), 0)),
            _const_spec((2, d)),
            _const_spec((d, 3 * d)),
            _const_spec((3, d)),
            _const_spec((d, d)),
        ],
        out_specs=pl.BlockSpec((1, tm, d), lambda b, i: (b, i, 0)),
        out_shape=jax.ShapeDtypeStruct((bsz, s, d), F32),
        scratch_shapes=[
            pltpu.VMEM((tm + 2 * CONV_HALO, d), BF16),
            pltpu.VMEM((tm + 2 * CONV_HALO, d), F32),
        ],
        compiler_params=_params(2),
        name="conv_mixer",
    )(x, x, x, norms2, w_in, conv_w, w_out)


def _ffn_kernel(x_ref, n_ref, win_ref, wout_ref, o_ref):
    x = x_ref[...]
    h = _rms(x, n_ref[0:1, :]).astype(BF16)
    gu = _dot(h, win_ref[...])
    gate = gu[:, :D_FF]
    up = gu[:, D_FF:]
    act = (gate * jax.nn.sigmoid(gate)) * up
    y = _dot(act.astype(BF16), wout_ref[...])
    o_ref[...] = x + _rms(y, n_ref[1:2, :])


def _ffn(x, norms2, w_in, w_out):
    bsz, s, d = x.shape
    t = bsz * s
    tm = min(TOKEN_TILE, t)
    out = pl.pallas_call(
        _ffn_kernel,
        grid=(t // tm,),
        in_specs=[
            pl.BlockSpec((tm, d), lambda i: (i, 0)),
            _const_spec((2, d)),
            _const_spec((d, 2 * D_FF)),
            _const_spec((D_FF, d)),
        ],
        out_specs=pl.BlockSpec((tm, d), lambda i: (i, 0)),
        out_shape=jax.ShapeDtypeStruct((t, d), F32),
        compiler_params=_params(1),
        name="swiglu_ffn",
    )(x.reshape(t, d), norms2, w_in, w_out)
    return out.reshape(bsz, s, d)


def _mlstm_in_kernel(x_ref, n_ref, wm_ref, wt_ref, bg_ref, q_ref, v_ref, o_ref, kt_ref, gt_ref):
    h = _rms(x_ref[0], n_ref[0:1, :]).astype(BF16)
    main = _dot(h, wm_ref[...])
    q_ref[0] = (main[:, :QK_W] * (QK_DIM ** -0.5)).astype(BF16)
    v_ref[0] = main[:, QK_W:QK_W + D_MODEL].astype(BF16)
    o_ref[0] = main[:, QK_W + D_MODEL:]
    tr = lax.dot_general(wt_ref[...], h, (((1,), (1,)), ((), ())), preferred_element_type=F32)
    kt_ref[0] = tr[:QK_W, :].astype(BF16)
    gt_ref[0] = tr[QK_W:, :] + bg_ref[...]


def _mlstm_in(x, norms1, w_main, w_tr, b_gate_col):
    bsz, s, d = x.shape
    tm = min(TOKEN_TILE, s)
    nt = s // tm
    ng = 4 * HEADS
    return pl.pallas_call(
        _mlstm_in_kernel,
        grid=(bsz, nt),
        in_specs=[
            pl.BlockSpec((1, tm, d), lambda b, i: (b, i, 0)),
            _const_spec((1, d)),
            _const_spec((d, QK_W + 2 * d)),
            _const_spec((QK_W + ng, d)),
            _const_spec((ng, 1)),
        ],
        out_specs=[
            pl.BlockSpec((1, tm, QK_W), lambda b, i: (b, i, 0)),
            pl.BlockSpec((1, tm, d), lambda b, i: (b, i, 0)),
            pl.BlockSpec((1, tm, d), lambda b, i: (b, i, 0)),
            pl.BlockSpec((1, QK_W, tm), lambda b, i: (b, 0, i)),
            pl.BlockSpec((1, ng, tm), lambda b, i: (b, 0, i)),
        ],
        out_shape=[
            jax.ShapeDtypeStruct((bsz, s, QK_W), BF16),
            jax.ShapeDtypeStruct((bsz, s, d), BF16),
            jax.ShapeDtypeStruct((bsz, s, d), F32),
            jax.ShapeDtypeStruct((bsz, QK_W, s), BF16),
            jax.ShapeDtypeStruct((bsz, ng, s), F32),
        ],
        compiler_params=_params(2),
        name="mlstm_in_proj",
    )(x, norms1, w_main, w_tr, b_gate_col)


def _log_sigmoid(x):
    return jnp.minimum(x, 0.0) - jnp.log1p(jnp.exp(-jnp.abs(x)))


def _gate_kernel(gt_ref, row_ref, col_ref, a_scr, cm_scr, bc_scr, src_scr, *, s):
    nc = s // CHUNK
    hd = HEADS
    g = gt_ref[0]
    pos = lax.broadcasted_iota(jnp.int32, (hd, s), 1) % CHUNK
    for dr in range(2):
        ig = g[2 * hd * dr:2 * hd * dr + hd, :]
        bc = _log_sigmoid(g[2 * hd * dr + hd:2 * hd * dr + 2 * hd, :])
        steps = []
        k = 1
        while k < CHUNK:
            steps.append((k, pos >= k) if dr == 0 else (s - k, pos < CHUNK - k))
            k *= 2
        for amt, ok in steps:
            bc = bc + jnp.where(ok, pltpu.roll(bc, amt, 1), 0.0)
        a = ig - bc
        cm = a
        for amt, ok in steps:
            cm = jnp.maximum(cm, jnp.where(ok, pltpu.roll(cm, amt, 1), -jnp.inf))
        rs = slice(hd * dr, hd * dr + hd)
        a_scr[rs, :] = a
        cm_scr[rs, :] = cm
        bc_scr[rs, :] = bc
        row_ref[0, rs, :] = a

    src_scr[3 * UNITS:, :] = jnp.zeros((CHUNK - 3 * UNITS, s), F32)
    end_lane = (CHUNK - 1, 0)

    def chunk_step(c, carry):
        new = []
        for dr in range(2):
            m_prev = carry[dr]
            cidx = c if dr == 0 else nc - 1 - c
            cs = pl.ds(pl.multiple_of(cidx * CHUNK, CHUNK), CHUNK)
            rs = slice(hd * dr, hd * dr + hd)
            a_t = a_scr[rs, cs]
            cm_t = cm_scr[rs, cs]
            bc_t = bc_scr[rs, cs]
            b_last = bc_t[:, end_lane[dr]:end_lane[dr] + 1]
            m_run = jnp.maximum(m_prev, cm_t)
            m_last = jnp.maximum(m_prev, jnp.max(a_t, axis=1, keepdims=True))
            row_ref[0, UNITS + hd * dr:UNITS + hd * dr + hd, cs] = jnp.exp(a_t - m_last)
            row_ref[0, 2 * UNITS + hd * dr:2 * UNITS + hd * dr + hd, cs] = jnp.broadcast_to(
                jnp.exp(m_prev - m_last), (hd, CHUNK))
            base = 3 * hd * dr
            src_scr[base:base + hd, cs] = m_run
            src_scr[base + hd:base + 2 * hd, cs] = jnp.exp(m_prev - m_run)
            src_scr[base + 2 * hd:base + 3 * hd, cs] = jnp.exp(-(bc_t + m_run))
            new.append(b_last + m_last)
        return tuple(new)

    zero = jnp.zeros((hd, 1), F32)
    lax.fori_loop(0, nc, chunk_step, (zero, zero))

    def transpose_step(c, carry):
        cs = pl.ds(pl.multiple_of(c * CHUNK, CHUNK), CHUNK)
        col_ref[0, cs, :] = src_scr[:, cs].T
        return carry

    lax.fori_loop(0, nc, transpose_step, 0)


def _gate_scan(gates_t):
    bsz, ng, s = gates_t.shape
    kern = functools.partial(_gate_kernel, s=s)
    return pl.pallas_call(
        kern,
        grid=(bsz,),
        in_specs=[pl.BlockSpec((1, ng, s), lambda b: (b, 0, 0))],
        out_specs=[
            pl.BlockSpec((1, 3 * UNITS, s), lambda b: (b, 0, 0)),
            pl.BlockSpec((1, s, CHUNK), lambda b: (b, 0, 0)),
        ],
        out_shape=[
            jax.ShapeDtypeStruct((bsz, 3 * UNITS, s), F32),
            jax.ShapeDtypeStruct((bsz, s, CHUNK), F32),
        ],
        scratch_shapes=[
            pltpu.VMEM((UNITS, s), F32),
            pltpu.VMEM((UNITS, s), F32),
            pltpu.VMEM((UNITS, s), F32),
            pltpu.VMEM((CHUNK, s), F32),
        ],
        compiler_params=_params(1),
        name="mlstm_gate_scan",
    )(gates_t)


def _mlstm_chunk(dr, q_ref, kt_ref, v_ref, row_ref, col_ref, h_ref, cn_ref, r0):
    hd = HEADS
    rs = slice(r0, r0 + CHUNK)
    rows = row_ref[0, :, rs]
    cols = col_ref[0, rs, :]
    t_idx = lax.broadcasted_iota(jnp.int32, (CHUNK, CHUNK), 0)
    s_idx = lax.broadcasted_iota(jnp.int32, (CHUNK, CHUNK), 1)
    causal = (s_idx <= t_idx) if dr == 0 else (s_idx >= t_idx)
    ones = jnp.ones((CHUNK, V_DIM), BF16)
    for p in range(hd // 2):
        ps = slice(p * CHUNK, (p + 1) * CHUNK)
        q_pair = q_ref[0, rs, ps]
        kt_pair = kt_ref[0, ps, rs]
        cn_pair = cn_ref[ps, :]
        cn_bf = cn_pair.astype(BF16)
        for hh in range(2):
            h = 2 * p + hh
            mine = (s_idx >= QK_DIM) if hh else (s_idx < QK_DIM)
            q_h = jnp.where(mine, q_pair, jnp.zeros_like(q_pair))
            scores = _dot(q_h, kt_pair)
            u = hd * dr + h
            a_row = rows[u:u + 1, :]
            ws_row = rows[UNITS + u:UNITS + u + 1, :]
            dec_row = rows[2 * UNITS + u:2 * UNITS + u + 1, :]
            cb = 3 * hd * dr + h
            m_col = cols[:, cb:cb + 1]
            wi_col = cols[:, cb + hd:cb + hd + 1]
            cl_col = cols[:, cb + 2 * hd:cb + 2 * hd + 1]
            w_intra = jnp.exp(jnp.where(causal, a_row - m_col, -jnp.inf))
            lhs = jnp.concatenate(
                [(scores * w_intra).astype(BF16), (q_h.astype(F32) * wi_col).astype(BF16)], axis=1)
            v1 = jnp.concatenate([v_ref[0, rs, h * V_DIM:(h + 1) * V_DIM], ones], axis=1)
            out = _dot(lhs, jnp.concatenate([v1, cn_bf], axis=0))
            num = out[:, :V_DIM]
            den = out[:, V_DIM:]
            h_ref[0, rs, h * V_DIM:(h + 1) * V_DIM] = num / jnp.maximum(jnp.abs(den), cl_col)
            hs = slice(h * QK_DIM, (h + 1) * QK_DIM)
            kw = (kt_ref[0, hs, rs].astype(F32) * ws_row).astype(BF16)
            decay = jnp.concatenate([dec_row, dec_row], axis=1)
            cn_ref[hs, :] = cn_pair[hh * QK_DIM:(hh + 1) * QK_DIM, :] * decay + _dot(kw, v1)


def _mlstm_kernel(qf_ref, ktf_ref, vf_ref, rowf_ref, colf_ref, qb_ref, ktb_ref, vb_ref, rowb_ref, colb_ref,
                  hf_ref, hb_ref, cnf_ref, cnb_ref, *, cps):
    @pl.when(pl.program_id(1) == 0)
    def _():
        cnf_ref[...] = jnp.zeros_like(cnf_ref)
        cnb_ref[...] = jnp.zeros_like(cnb_ref)

    for c in range(cps):
        _mlstm_chunk(0, qf_ref, ktf_ref, vf_ref, rowf_ref, colf_ref, hf_ref, cnf_ref, c * CHUNK)
        _mlstm_chunk(1, qb_ref, ktb_ref, vb_ref, rowb_ref, colb_ref, hb_ref, cnb_ref, (cps - 1 - c) * CHUNK)


def _mlstm_scan(q, kt, v, rows, cols):
    bsz, s, d = v.shape
    cps = min(CHUNKS_PER_STEP, s // CHUNK)
    tt = cps * CHUNK
    nb = s // tt
    fwd = lambda b, j: j
    bwd = lambda b, j: nb - 1 - j

    def specs(pos):
        return [
            pl.BlockSpec((1, tt, QK_W), lambda b, j: (b, pos(b, j), 0)),
            pl.BlockSpec((1, QK_W, tt), lambda b, j: (b, 0, pos(b, j))),
            pl.BlockSpec((1, tt, d), lambda b, j: (b, pos(b, j), 0)),
            pl.BlockSpec((1, 3 * UNITS, tt), lambda b, j: (b, 0, pos(b, j))),
            pl.BlockSpec((1, tt, CHUNK), lambda b, j: (b, pos(b, j), 0)),
        ]

    kern = functools.partial(_mlstm_kernel, cps=cps)
    return pl.pallas_call(
        kern,
        grid=(bsz, nb),
        in_specs=specs(fwd) + specs(bwd),
        out_specs=[
            pl.BlockSpec((1, tt, d), lambda b, j: (b, j, 0)),
            pl.BlockSpec((1, tt, d), lambda b, j: (b, nb - 1 - j, 0)),
        ],
        out_shape=[jax.ShapeDtypeStruct((bsz, s, d), F32), jax.ShapeDtypeStruct((bsz, s, d), F32)],
        scratch_shapes=[
            pltpu.VMEM((HEADS * QK_DIM, 2 * V_DIM), F32),
            pltpu.VMEM((HEADS * QK_DIM, 2 * V_DIM), F32),
        ],
        compiler_params=_params(2),
        name="mlstm_scan",
    )(q, kt, v, rows, cols, q, kt, v, rows, cols)


def _mlstm_out_kernel(x_ref, hf_ref, hb_ref, o_ref, n_ref, ng_ref, wout_ref, y_ref):
    h = hf_ref[...] + hb_ref[...]
    parts = []
    for hd in range(HEADS):
        hh = h[:, hd * V_DIM:(hd + 1) * V_DIM]
        parts.append(hh * lax.rsqrt(jnp.mean(hh * hh, axis=-1, keepdims=True) + EPS))
    hn = jnp.concatenate(parts, axis=1) * ng_ref[...]
    gated = (jax.nn.sigmoid(o_ref[...]) * hn).astype(BF16)
    mix = _dot(gated, wout_ref[...])
    y_ref[...] = x_ref[...] + _rms(mix, n_ref[0:1, :])


def _mlstm_out(x, hf, hb, o, norm_post, norm_g, w_out):
    bsz, s, d = x.shape
    t = bsz * s
    tm = min(TOKEN_TILE, t)
    tok = pl.BlockSpec((tm, d), lambda i: (i, 0))
    out = pl.pallas_call(
        _mlstm_out_kernel,
        grid=(t // tm,),
        in_specs=[tok, tok, tok, tok, _const_spec((1, d)), _const_spec((1, d)), _const_spec((d, d))],
        out_specs=tok,
        out_shape=jax.ShapeDtypeStruct((t, d), F32),
        compiler_params=_params(1),
        name="mlstm_out_proj",
    )(x.reshape(t, d), hf.reshape(t, d), hb.reshape(t, d), o.reshape(t, d), norm_post, norm_g, w_out)
    return out.reshape(bsz, s, d)


def _trunk(x, p):
    x = _conv_mixer(x, p["n0_mix"], p["conv_w_in"], p["conv_w"], p["conv_w_out"])
    x = _ffn(x, p["n0_ffn"], p["ffn_w_in0"], p["ffn_w_out0"])
    q, v, o, kt, gt = _mlstm_in(x, p["n1_in"], p["ml_w_main"], p["ml_w_tr"], p["ml_b_gate"])
    rows, cols = _gate_scan(gt)
    hf, hb = _mlstm_scan(q, kt, v, rows, cols)
    x = _mlstm_out(x, hf, hb, o, p["n1_post"], p["ml_norm"], p["ml_w_out"])
    x = _ffn(x, p["n1_ffn"], p["ffn_w_in1"], p["ffn_w_out1"])
    return x


def kernel(x_prompt, x_sample, norms, conv_w_in, conv_w, conv_w_out, mlstm_w_in, mlstm_b_gate, mlstm_norm,
           mlstm_w_out, ffn_w_in, ffn_w_out):
    d = D_MODEL
    w_ml = mlstm_w_in[0]
    w_q, w_k = w_ml[:, :QK_W], w_ml[:, QK_W:2 * QK_W]
    w_vo = w_ml[:, 2 * QK_W:2 * QK_W + 2 * d]
    w_g = w_ml[:, 2 * QK_W + 2 * d:]
    p = {
        "n0_mix": norms[0, 0:2],
        "n0_ffn": norms[0, 2:4],
        "n1_in": norms[1, 0:1],
        "n1_post": norms[1, 1:2],
        "n1_ffn": norms[1, 2:4],
        "conv_w_in": conv_w_in[0].astype(BF16),
        "conv_w": conv_w[0],
        "conv_w_out": conv_w_out[0].astype(BF16),
        "ffn_w_in0": ffn_w_in[0].astype(BF16),
        "ffn_w_out0": ffn_w_out[0].astype(BF16),
        "ffn_w_in1": ffn_w_in[1].astype(BF16),
        "ffn_w_out1": ffn_w_out[1].astype(BF16),
        "ml_w_main": jnp.concatenate([w_q, w_vo], axis=1).astype(BF16),
        "ml_w_tr": jnp.concatenate([w_k, w_g], axis=1).T.astype(BF16),
        "ml_b_gate": mlstm_b_gate[0].reshape(4 * HEADS, 1),
        "ml_norm": mlstm_norm[0].reshape(1, d),
        "ml_w_out": mlstm_w_out[0].astype(BF16),
    }
    return (_trunk(x_prompt, p), _trunk(x_sample, p))
```

```python
import functools

import jax
import jax.numpy as jnp
from jax import lax
from jax.experimental import pallas as pl
from jax.experimental.pallas import tpu as pltpu

F32 = jnp.float32
BF16 = jnp.bfloat16

D_MODEL = 1024
HEADS = 8
QK_DIM = 64
V_DIM = 128
QK_W = HEADS * QK_DIM
CHUNK = 128
D_FF = 2816
EPS = 1e-6
CONV_HALO = 16
UNITS = 2 * HEADS

VMEM_LIMIT_BYTES = 56 * 1024 * 1024

TOKEN_TILE = 512
CHUNKS_PER_STEP = 4


def _rms(x, g):
    ms = jnp.mean(x * x, axis=-1, keepdims=True)
    return x * lax.rsqrt(ms + EPS) * g


def _dot(a, b):
    return jnp.dot(a, b, preferred_element_type=F32)


def _const_spec(shape):
    nd = len(shape)
    return pl.BlockSpec(shape, lambda *_: (0,) * nd, pipeline_mode=pl.Buffered(1))


def _params(n_axes):
    return pltpu.CompilerParams(
        dimension_semantics=("arbitrary",) * n_axes,
        vmem_limit_bytes=VMEM_LIMIT_BYTES,
    )


def _conv_kernel(x_ref, xp_ref, xn_ref, n_ref, win_ref, cw_ref, wout_ref, o_ref, h_scr, u_scr, *, tm, nt):
    i = pl.program_id(1)
    d = D_MODEL
    hl = CONV_HALO
    g_in = n_ref[0:1, :]
    x = x_ref[0]
    h_scr[0:hl, :] = _rms(xp_ref[0], g_in).astype(BF16)
    h_scr[hl:hl + tm, :] = _rms(x, g_in).astype(BF16)
    h_scr[hl + tm:, :] = _rms(xn_ref[0], g_in).astype(BF16)
    cv = _dot(h_scr[...], win_ref[:, d:3 * d])
    u_scr[...] = cv[:, :d] * cv[:, d:]

    @pl.when(i == 0)
    def _():
        u_scr[0:hl, :] = jnp.zeros((hl, d), F32)

    @pl.when(i == nt - 1)
    def _():
        u_scr[hl + tm:, :] = jnp.zeros((hl, d), F32)

    conv = (u_scr[hl - 1:hl - 1 + tm, :] * cw_ref[0:1, :]
            + u_scr[hl:hl + tm, :] * cw_ref[1:2, :]
            + u_scr[hl + 1:hl + 1 + tm, :] * cw_ref[2:3, :])
    b = _dot(h_scr[hl:hl + tm, :], win_ref[:, 0:d])
    mix = _dot((b * conv).astype(BF16), wout_ref[...])
    o_ref[0] = x + _rms(mix, n_ref[1:2, :])


def _conv_mixer(x, norms2, w_in, conv_w, w_out):
    bsz, s, d = x.shape
    tm = min(TOKEN_TILE, s)
    nt = s // tm
    hb = tm // CONV_HALO
    last_hb = s // CONV_HALO - 1
    kern = functools.partial(_conv_kernel, tm=tm, nt=nt)
    return pl.pallas_call(
        kern,
        grid=(bsz, nt),
        in_specs=[
            pl.BlockSpec((1, tm, d), lambda b, i: (b, i, 0)),
            pl.BlockSpec((1, CONV_HALO, d), lambda b, i: (b, jnp.maximum(i * hb - 1, 0), 0)),
            pl.BlockSpec((1, CONV_HALO, d), lambda b, i: (b, jnp.minimum((i + 1) * hb, last_hb), 0)),
            _const_spec((2, d)),
            _const_spec((d, 3 * d)),
            _const_spec((3, d)),
            _const_spec((d, d)),
        ],
        out_specs=pl.BlockSpec((1, tm, d), lambda b, i: (b, i, 0)),
        out_shape=jax.ShapeDtypeStruct((bsz, s, d), F32),
        scratch_shapes=[
            pltpu.VMEM((tm + 2 * CONV_HALO, d), BF16),
            pltpu.VMEM((tm + 2 * CONV_HALO, d), F32),
        ],
        compiler_params=_params(2),
        name="conv_mixer",
    )(x, x, x, norms2, w_in, conv_w, w_out)


def _ffn_body(x, norms2, win_ref, wout_ref):
    h = _rms(x, norms2[0:1, :]).astype(BF16)
    gu = _dot(h, win_ref[...])
    gate = gu[:, :D_FF]
    up = gu[:, D_FF:]
    act = (gate * jax.nn.sigmoid(gate)) * up
    y = _dot(act.astype(BF16), wout_ref[...])
    return x + _rms(y, norms2[1:2, :])


def _ffn_kernel(x_ref, n_ref, win_ref, wout_ref, o_ref):
    o_ref[...] = _ffn_body(x_ref[...], n_ref[...], win_ref, wout_ref)


def _ffn(x, norms2, w_in, w_out):
    bsz, s, d = x.shape
    t = bsz * s
    tm = min(TOKEN_TILE, t)
    out = pl.pallas_call(
        _ffn_kernel,
        grid=(t // tm,),
        in_specs=[
            pl.BlockSpec((tm, d), lambda i: (i, 0)),
            _const_spec((2, d)),
            _const_spec((d, 2 * D_FF)),
            _const_spec((D_FF, d)),
        ],
        out_specs=pl.BlockSpec((tm, d), lambda i: (i, 0)),
        out_shape=jax.ShapeDtypeStruct((t, d), F32),
        compiler_params=_params(1),
        name="swiglu_ffn",
    )(x.reshape(t, d), norms2, w_in, w_out)
    return out.reshape(bsz, s, d)


def _mlstm_in_kernel(x_ref, n_ref, wm_ref, wt_ref, bg_ref, q_ref, v_ref, o_ref, kt_ref, gt_ref):
    h = _rms(x_ref[0], n_ref[0:1, :]).astype(BF16)
    main = _dot(h, wm_ref[...])
    q_ref[0] = (main[:, :QK_W] * (QK_DIM ** -0.5)).astype(BF16)
    v_ref[0] = main[:, QK_W:QK_W + D_MODEL].astype(BF16)
    o_ref[0] = main[:, QK_W + D_MODEL:]
    tr = lax.dot_general(wt_ref[...], h, (((1,), (1,)), ((), ())), preferred_element_type=F32)
    kt_ref[0] = tr[:QK_W, :].astype(BF16)
    gt_ref[0] = tr[QK_W:, :] + bg_ref[...]


def _mlstm_in(x, norms1, w_main, w_tr, b_gate_col):
    bsz, s, d = x.shape
    tm = min(TOKEN_TILE, s)
    nt = s // tm
    ng = 4 * HEADS
    return pl.pallas_call(
        _mlstm_in_kernel,
        grid=(bsz, nt),
        in_specs=[
            pl.BlockSpec((1, tm, d), lambda b, i: (b, i, 0)),
            _const_spec((1, d)),
            _const_spec((d, QK_W + 2 * d)),
            _const_spec((QK_W + ng, d)),
            _const_spec((ng, 1)),
        ],
        out_specs=[
            pl.BlockSpec((1, tm, QK_W), lambda b, i: (b, i, 0)),
            pl.BlockSpec((1, tm, d), lambda b, i: (b, i, 0)),
            pl.BlockSpec((1, tm, d), lambda b, i: (b, i, 0)),
            pl.BlockSpec((1, QK_W, tm), lambda b, i: (b, 0, i)),
            pl.BlockSpec((1, ng, tm), lambda b, i: (b, 0, i)),
        ],
        out_shape=[
            jax.ShapeDtypeStruct((bsz, s, QK_W), BF16),
            jax.ShapeDtypeStruct((bsz, s, d), BF16),
            jax.ShapeDtypeStruct((bsz, s, d), F32),
            jax.ShapeDtypeStruct((bsz, QK_W, s), BF16),
            jax.ShapeDtypeStruct((bsz, ng, s), F32),
        ],
        compiler_params=_params(2),
        name="mlstm_in_proj",
    )(x, norms1, w_main, w_tr, b_gate_col)


def _log_sigmoid(x):
    return jnp.minimum(x, 0.0) - jnp.log1p(jnp.exp(-jnp.abs(x)))


def _gate_kernel(gt_ref, row_ref, col_ref, a_scr, cm_scr, bc_scr, am_scr, bl_scr, mp_scr, src_scr, *, s):
    nc = s // CHUNK
    hd = HEADS
    g = gt_ref[0]
    pos = lax.broadcasted_iota(jnp.int32, (hd, s), 1) % CHUNK
    prefix, suffix = [], []
    k = 1
    while k < CHUNK:
        prefix.append((k, pos >= k))
        suffix.append((s - k, pos < CHUNK - k))
        k *= 2

    def scan(x, steps, op, fill):
        for amt, ok in steps:
            x = op(x, jnp.where(ok, pltpu.roll(x, amt, 1), fill))
        return x

    for dr in range(2):
        along, against = (prefix, suffix) if dr == 0 else (suffix, prefix)
        end = CHUNK - 1 if dr == 0 else 0
        ig = g[2 * hd * dr:2 * hd * dr + hd, :]
        bc = scan(_log_sigmoid(g[2 * hd * dr + hd:2 * hd * dr + 2 * hd, :]), along, jnp.add, 0.0)
        a = ig - bc
        cm = scan(a, along, jnp.maximum, -jnp.inf)
        rs = slice(hd * dr, hd * dr + hd)
        a_scr[rs, :] = a
        cm_scr[rs, :] = cm
        bc_scr[rs, :] = bc
        am_scr[rs, :] = scan(cm, against, jnp.maximum, -jnp.inf)
        bl_scr[rs, :] = scan(jnp.where(pos == end, bc, 0.0), against, jnp.add, 0.0)

    def chunk_step(c, carry):
        new = []
        for dr in range(2):
            cidx = c if dr == 0 else nc - 1 - c
            cs = pl.ds(pl.multiple_of(cidx * CHUNK, CHUNK), CHUNK)
            rs = slice(hd * dr, hd * dr + hd)
            mp_scr[rs, cs] = carry[dr]
            new.append(bl_scr[rs, cs] + jnp.maximum(carry[dr], am_scr[rs, cs]))
        return tuple(new)

    zero = jnp.zeros((hd, CHUNK), F32)
    lax.fori_loop(0, nc, chunk_step, (zero, zero))

    m_prev = mp_scr[...]
    m_run = jnp.maximum(m_prev, cm_scr[...])
    m_last = jnp.maximum(m_prev, am_scr[...])
    row_ref[0, 0:UNITS, :] = a_scr[...]
    row_ref[0, UNITS:2 * UNITS, :] = jnp.exp(a_scr[...] - m_last)
    row_ref[0, 2 * UNITS:3 * UNITS, :] = jnp.exp(m_prev - m_last)
    w_inter = jnp.exp(m_prev - m_run)
    clamp = jnp.exp(-(bc_scr[...] + m_run))
    for dr in range(2):
        rs = slice(hd * dr, hd * dr + hd)
        base = 3 * hd * dr
        src_scr[base:base + hd, :] = m_run[rs, :]
        src_scr[base + hd:base + 2 * hd, :] = w_inter[rs, :]
        src_scr[base + 2 * hd:base + 3 * hd, :] = clamp[rs, :]
    src_scr[3 * UNITS:, :] = jnp.zeros((CHUNK - 3 * UNITS, s), F32)

    def transpose_step(c, carry):
        cs = pl.ds(pl.multiple_of(c * CHUNK, CHUNK), CHUNK)
        col_ref[0, cs, :] = src_scr[:, cs].T
        return carry

    lax.fori_loop(0, nc, transpose_step, 0, unroll=4)


def _gate_scan(gates_t):
    bsz, ng, s = gates_t.shape
    kern = functools.partial(_gate_kernel, s=s)
    return pl.pallas_call(
        kern,
        grid=(bsz,),
        in_specs=[pl.BlockSpec((1, ng, s), lambda b: (b, 0, 0))],
        out_specs=[
            pl.BlockSpec((1, 3 * UNITS, s), lambda b: (b, 0, 0)),
            pl.BlockSpec((1, s, CHUNK), lambda b: (b, 0, 0)),
        ],
        out_shape=[
            jax.ShapeDtypeStruct((bsz, 3 * UNITS, s), F32),
            jax.ShapeDtypeStruct((bsz, s, CHUNK), F32),
        ],
        scratch_shapes=[
            pltpu.VMEM((UNITS, s), F32),
            pltpu.VMEM((UNITS, s), F32),
            pltpu.VMEM((UNITS, s), F32),
            pltpu.VMEM((UNITS, s), F32),
            pltpu.VMEM((UNITS, s), F32),
            pltpu.VMEM((UNITS, s), F32),
            pltpu.VMEM((CHUNK, s), F32),
        ],
        compiler_params=_params(1),
        name="mlstm_gate_scan",
    )(gates_t)


def _mlstm_chunk(dr, q_ref, kt_ref, v_ref, row_ref, col_ref, h_ref, cn_ref, r0):
    hd = HEADS
    rs = slice(r0, r0 + CHUNK)
    rows = row_ref[0, :, rs]
    cols = col_ref[0, rs, :]
    t_idx = lax.broadcasted_iota(jnp.int32, (CHUNK, CHUNK), 0)
    s_idx = lax.broadcasted_iota(jnp.int32, (CHUNK, CHUNK), 1)
    causal = (s_idx <= t_idx) if dr == 0 else (s_idx >= t_idx)
    ones = jnp.ones((CHUNK, V_DIM), BF16)
    for p in range(hd // 2):
        ps = slice(p * CHUNK, (p + 1) * CHUNK)
        q_pair = q_ref[0, rs, ps]
        kt_pair = kt_ref[0, ps, rs]
        cn_pair = cn_ref[ps, :]
        cn_bf = cn_pair.astype(BF16)
        for hh in range(2):
            h = 2 * p + hh
            mine = (s_idx >= QK_DIM) if hh else (s_idx < QK_DIM)
            q_h = jnp.where(mine, q_pair, jnp.zeros_like(q_pair))
            scores = _dot(q_h, kt_pair)
            u = hd * dr + h
            a_row = rows[u:u + 1, :]
            ws_row = rows[UNITS + u:UNITS + u + 1, :]
            dec_row = rows[2 * UNITS + u:2 * UNITS + u + 1, :]
            cb = 3 * hd * dr + h
            m_col = cols[:, cb:cb + 1]
            wi_col = cols[:, cb + hd:cb + hd + 1]
            cl_col = cols[:, cb + 2 * hd:cb + 2 * hd + 1]
            w_intra = jnp.exp(jnp.where(causal, a_row - m_col, -jnp.inf))
            lhs = jnp.concatenate(
                [(scores * w_intra).astype(BF16), (q_h.astype(F32) * wi_col).astype(BF16)], axis=1)
            v1 = jnp.concatenate([v_ref[0, rs, h * V_DIM:(h + 1) * V_DIM], ones], axis=1)
            out = _dot(lhs, jnp.concatenate([v1, cn_bf], axis=0))
            num = out[:, :V_DIM]
            den = out[:, V_DIM:]
            h_ref[0, rs, h * V_DIM:(h + 1) * V_DIM] = num / jnp.maximum(jnp.abs(den), cl_col)
            hs = slice(h * QK_DIM, (h + 1) * QK_DIM)
            kw = (kt_ref[0, hs, rs].astype(F32) * ws_row).astype(BF16)
            decay = jnp.concatenate([dec_row, dec_row], axis=1)
            cn_ref[hs, :] = cn_pair[hh * QK_DIM:(hh + 1) * QK_DIM, :] * decay + _dot(kw, v1)


def _mlstm_kernel(qf_ref, ktf_ref, vf_ref, rowf_ref, colf_ref, qb_ref, ktb_ref, vb_ref, rowb_ref, colb_ref,
                  hf_ref, hb_ref, cnf_ref, cnb_ref, *, cps):
    @pl.when(pl.program_id(1) == 0)
    def _():
        cnf_ref[...] = jnp.zeros_like(cnf_ref)
        cnb_ref[...] = jnp.zeros_like(cnb_ref)

    for c in range(cps):
        _mlstm_chunk(0, qf_ref, ktf_ref, vf_ref, rowf_ref, colf_ref, hf_ref, cnf_ref, c * CHUNK)
        _mlstm_chunk(1, qb_ref, ktb_ref, vb_ref, rowb_ref, colb_ref, hb_ref, cnb_ref, (cps - 1 - c) * CHUNK)


def _mlstm_scan(q, kt, v, rows, cols):
    bsz, s, d = v.shape
    cps = min(CHUNKS_PER_STEP, s // CHUNK)
    tt = cps * CHUNK
    nb = s // tt
    fwd = lambda b, j: j
    bwd = lambda b, j: nb - 1 - j

    def specs(pos):
        return [
            pl.BlockSpec((1, tt, QK_W), lambda b, j: (b, pos(b, j), 0)),
            pl.BlockSpec((1, QK_W, tt), lambda b, j: (b, 0, pos(b, j))),
            pl.BlockSpec((1, tt, d), lambda b, j: (b, pos(b, j), 0)),
            pl.BlockSpec((1, 3 * UNITS, tt), lambda b, j: (b, 0, pos(b, j))),
            pl.BlockSpec((1, tt, CHUNK), lambda b, j: (b, pos(b, j), 0)),
        ]

    kern = functools.partial(_mlstm_kernel, cps=cps)
    return pl.pallas_call(
        kern,
        grid=(bsz, nb),
        in_specs=specs(fwd) + specs(bwd),
        out_specs=[
            pl.BlockSpec((1, tt, d), lambda b, j: (b, j, 0)),
            pl.BlockSpec((1, tt, d), lambda b, j: (b, nb - 1 - j, 0)),
        ],
        out_shape=[jax.ShapeDtypeStruct((bsz, s, d), F32), jax.ShapeDtypeStruct((bsz, s, d), F32)],
        scratch_shapes=[
            pltpu.VMEM((HEADS * QK_DIM, 2 * V_DIM), F32),
            pltpu.VMEM((HEADS * QK_DIM, 2 * V_DIM), F32),
        ],
        compiler_params=_params(2),
        name="mlstm_scan",
    )(q, kt, v, rows, cols, q, kt, v, rows, cols)


def _mlstm_out_ffn_kernel(x_ref, hf_ref, hb_ref, o_ref, n_ref, ng_ref, wmix_ref, win_ref, wout_ref, y_ref):
    h = hf_ref[...] + hb_ref[...]
    parts = []
    for hd in range(HEADS):
        hh = h[:, hd * V_DIM:(hd + 1) * V_DIM]
        parts.append(hh * lax.rsqrt(jnp.mean(hh * hh, axis=-1, keepdims=True) + EPS))
    hn = jnp.concatenate(parts, axis=1) * ng_ref[...]
    gated = (jax.nn.sigmoid(o_ref[...]) * hn).astype(BF16)
    x = x_ref[...] + _rms(_dot(gated, wmix_ref[...]), n_ref[0:1, :])
    y_ref[...] = _ffn_body(x, n_ref[1:3, :], win_ref, wout_ref)


def _mlstm_out_ffn(x, hf, hb, o, norms3, norm_g, w_mix, w_in, w_out):
    bsz, s, d = x.shape
    t = bsz * s
    tm = min(TOKEN_TILE, t)
    tok = pl.BlockSpec((tm, d), lambda i: (i, 0))
    out = pl.pallas_call(
        _mlstm_out_ffn_kernel,
        grid=(t // tm,),
        in_specs=[tok, tok, tok, tok, _const_spec((3, d)), _const_spec((1, d)), _const_spec((d, d)),
                  _const_spec((d, 2 * D_FF)), _const_spec((D_FF, d))],
        out_specs=tok,
        out_shape=jax.ShapeDtypeStruct((t, d), F32),
        compiler_params=_params(1),
        name="mlstm_out_ffn",
    )(x.reshape(t, d), hf.reshape(t, d), hb.reshape(t, d), o.reshape(t, d), norms3, norm_g, w_mix, w_in, w_out)
    return out.reshape(bsz, s, d)


def _trunk(x, p):
    x = _conv_mixer(x, p["n0_mix"], p["conv_w_in"], p["conv_w"], p["conv_w_out"])
    x = _ffn(x, p["n0_ffn"], p["ffn_w_in0"], p["ffn_w_out0"])
    q, v, o, kt, gt = _mlstm_in(x, p["n1_in"], p["ml_w_main"], p["ml_w_tr"], p["ml_b_gate"])
    rows, cols = _gate_scan(gt)
    hf, hb = _mlstm_scan(q, kt, v, rows, cols)
    return _mlstm_out_ffn(x, hf, hb, o, p["n1_post_ffn"], p["ml_norm"], p["ml_w_out"],
                          p["ffn_w_in1"], p["ffn_w_out1"])


def kernel(x_prompt, x_sample, norms, conv_w_in, conv_w, conv_w_out, mlstm_w_in, mlstm_b_gate, mlstm_norm,
           mlstm_w_out, ffn_w_in, ffn_w_out):
    d = D_MODEL
    w_ml = mlstm_w_in[0]
    w_q, w_k = w_ml[:, :QK_W], w_ml[:, QK_W:2 * QK_W]
    w_vo = w_ml[:, 2 * QK_W:2 * QK_W + 2 * d]
    w_g = w_ml[:, 2 * QK_W + 2 * d:]
    p = {
        "n0_mix": norms[0, 0:2],
        "n0_ffn": norms[0, 2:4],
        "n1_in": norms[1, 0:1],
        "n1_post_ffn": norms[1, 1:4],
        "conv_w_in": conv_w_in[0].astype(BF16),
        "conv_w": conv_w[0],
        "conv_w_out": conv_w_out[0].astype(BF16),
        "ffn_w_in0": ffn_w_in[0].astype(BF16),
        "ffn_w_out0": ffn_w_out[0].astype(BF16),
        "ffn_w_in1": ffn_w_in[1].astype(BF16),
        "ffn_w_out1": ffn_w_out[1].astype(BF16),
        "ml_w_main": jnp.concatenate([w_q, w_vo], axis=1).astype(BF16),
        "ml_w_tr": jnp.concatenate([w_k, w_g], axis=1).T.astype(BF16),
        "ml_b_gate": mlstm_b_gate[0].reshape(4 * HEADS, 1),
        "ml_norm": mlstm_norm[0].reshape(1, d),
        "ml_w_out": mlstm_w_out[0].astype(BF16),
    }
    return (_trunk(x_prompt, p), _trunk(x_sample, p))
```

```python
import functools

import jax
import jax.numpy as jnp
from jax import lax
from jax.experimental import pallas as pl
from jax.experimental.pallas import tpu as pltpu

F32 = jnp.float32
BF16 = jnp.bfloat16

D_MODEL = 1024
HEADS = 8
QK_DIM = 64
V_DIM = 128
QK_W = HEADS * QK_DIM
CHUNK = 128
D_FF = 2816
EPS = 1e-6
CONV_HALO = 16
UNITS = 2 * HEADS

VMEM_LIMIT_BYTES = 56 * 1024 * 1024

TOKEN_TILE = 512
WIDE_TILE = 1024
ROW_GROUP = 256
ROW_PIECE = 32
FF_CHUNK = 256
CHUNKS_PER_STEP = 4


def _rms(x, g):
    ms = jnp.mean(x * x, axis=-1, keepdims=True)
    return x * lax.rsqrt(ms + EPS) * g


def _dot(a, b):
    return jnp.dot(a, b, preferred_element_type=F32)


def _const_spec(shape):
    nd = len(shape)
    return pl.BlockSpec(shape, lambda *_: (0,) * nd, pipeline_mode=pl.Buffered(1))


def _params(n_axes):
    return pltpu.CompilerParams(
        dimension_semantics=("arbitrary",) * n_axes,
        vmem_limit_bytes=VMEM_LIMIT_BYTES,
    )


def _software_pipeline(n_sub, head_tasks, main_tasks, tail_tasks):
    for task in head_tasks(0):
        task()
    for s in range(n_sub):
        side = (head_tasks(s + 1) if s + 1 < n_sub else []) + (tail_tasks(s - 1) if s > 0 else [])
        main = main_tasks(s)
        done = 0
        for i, task in enumerate(main):
            task()
            upto = (i + 1) * len(side) // len(main)
            for piece in side[done:upto]:
                piece()
            done = upto
    for task in tail_tasks(n_sub - 1):
        task()


def _conv_kernel(x_ref, xp_ref, xn_ref, n_ref, win_ref, cw_ref, wout_ref, o_ref, h_scr, u_scr, *, tm, nt):
    i = pl.program_id(1)
    d = D_MODEL
    hl = CONV_HALO
    g_in = n_ref[0:1, :]
    x = x_ref[0]
    h_scr[0:hl, :] = _rms(xp_ref[0], g_in).astype(BF16)
    h_scr[hl:hl + tm, :] = _rms(x, g_in).astype(BF16)
    h_scr[hl + tm:, :] = _rms(xn_ref[0], g_in).astype(BF16)
    cv = _dot(h_scr[...], win_ref[:, d:3 * d])
    u_scr[...] = cv[:, :d] * cv[:, d:]

    @pl.when(i == 0)
    def _():
        u_scr[0:hl, :] = jnp.zeros((hl, d), F32)

    @pl.when(i == nt - 1)
    def _():
        u_scr[hl + tm:, :] = jnp.zeros((hl, d), F32)

    conv = (u_scr[hl - 1:hl - 1 + tm, :] * cw_ref[0:1, :]
            + u_scr[hl:hl + tm, :] * cw_ref[1:2, :]
            + u_scr[hl + 1:hl + 1 + tm, :] * cw_ref[2:3, :])
    b = _dot(h_scr[hl:hl + tm, :], win_ref[:, 0:d])
    mix = _dot((b * conv).astype(BF16), wout_ref[...])
    o_ref[0] = x + _rms(mix, n_ref[1:2, :])


def _conv_mixer(x, norms2, w_in, conv_w, w_out):
    bsz, s, d = x.shape
    tm = min(WIDE_TILE, s)
    nt = s // tm
    hb = tm // CONV_HALO
    last_hb = s // CONV_HALO - 1
    kern = functools.partial(_conv_kernel, tm=tm, nt=nt)
    return pl.pallas_call(
        kern,
        grid=(bsz, nt),
        in_specs=[
            pl.BlockSpec((1, tm, d), lambda b, i: (b, i, 0)),
            pl.BlockSpec((1, CONV_HALO, d), lambda b, i: (b, jnp.maximum(i * hb - 1, 0), 0)),
            pl.BlockSpec((1, CONV_HALO, d), lambda b, i: (b, jnp.minimum((i + 1) * hb, last_hb), 0)),
            _const_spec((2, d)),
            _const_spec((d, 3 * d)),
            _const_spec((3, d)),
            _const_spec((d, d)),
        ],
        out_specs=pl.BlockSpec((1, tm, d), lambda b, i: (b, i, 0)),
        out_shape=jax.ShapeDtypeStruct((bsz, s, d), F32),
        scratch_shapes=[
            pltpu.VMEM((tm + 2 * CONV_HALO, d), BF16),
            pltpu.VMEM((tm + 2 * CONV_HALO, d), F32),
        ],
        compiler_params=_params(2),
        name="conv_mixer",
    )(x, x, x, norms2, w_in, conv_w, w_out)


def _ffn_tasks(sub, piece, x_of, out_ref, norms2, win_ref, wout_ref, h_scr, act_scr, y_scr):
    n_piece = sub // piece
    n_in = D_FF // FF_CHUNK
    n_out = D_MODEL // FF_CHUNK

    def rows_of(s, r):
        return slice(s * sub + r * piece, s * sub + (r + 1) * piece)

    def head(s):
        def norm(r):
            def run():
                h_scr[s % 2, r * piece:(r + 1) * piece, :] = _rms(x_of(s, r), norms2[0:1, :]).astype(BF16)
            return run
        return [norm(r) for r in range(n_piece)]

    def main(s):
        def up_chunk(c):
            def run():
                cs = slice(c * FF_CHUNK, (c + 1) * FF_CHUNK)
                h = h_scr[s % 2]
                gate = _dot(h, win_ref[:, cs])
                up = _dot(h, win_ref[:, D_FF + c * FF_CHUNK:D_FF + (c + 1) * FF_CHUNK])
                act_scr[s % 2, :, cs] = ((gate * jax.nn.sigmoid(gate)) * up).astype(BF16)
            return run

        def down_chunk(c):
            def run():
                cs = slice(c * FF_CHUNK, (c + 1) * FF_CHUNK)
                y_scr[s % 2, :, cs] = _dot(act_scr[s % 2], wout_ref[:, cs])
            return run
        return [up_chunk(c) for c in range(n_in)] + [down_chunk(c) for c in range(n_out)]

    def tail(s):
        def finish(r):
            def run():
                y = y_scr[s % 2, r * piece:(r + 1) * piece, :]
                out_ref[rows_of(s, r), :] = x_of(s, r) + _rms(y, norms2[1:2, :])
            return run
        return [finish(r) for r in range(n_piece)]

    return head, main, tail


def _ffn_kernel(x_ref, n_ref, win_ref, wout_ref, o_ref, h_scr, act_scr, y_scr, *, sub):
    piece = ROW_PIECE
    x_of = lambda s, r: x_ref[s * sub + r * piece:s * sub + (r + 1) * piece, :]
    head, main, tail = _ffn_tasks(sub, piece, x_of, o_ref, n_ref[...], win_ref, wout_ref, h_scr, act_scr, y_scr)
    _software_pipeline(x_ref.shape[0] // sub, head, main, tail)


def _ffn_scratch(sub):
    return [pltpu.VMEM((2, sub, D_MODEL), BF16), pltpu.VMEM((2, sub, D_FF), BF16), pltpu.VMEM((2, sub, D_MODEL), F32)]


def _ffn(x, norms2, w_in, w_out):
    bsz, s, d = x.shape
    t = bsz * s
    tm = min(WIDE_TILE, t)
    sub = min(ROW_GROUP, tm)
    out = pl.pallas_call(
        functools.partial(_ffn_kernel, sub=sub),
        grid=(t // tm,),
        in_specs=[
            pl.BlockSpec((tm, d), lambda i: (i, 0)),
            _const_spec((2, d)),
            _const_spec((d, 2 * D_FF)),
            _const_spec((D_FF, d)),
        ],
        out_specs=pl.BlockSpec((tm, d), lambda i: (i, 0)),
        out_shape=jax.ShapeDtypeStruct((t, d), F32),
        scratch_shapes=_ffn_scratch(sub),
        compiler_params=_params(1),
        name="swiglu_ffn",
    )(x.reshape(t, d), norms2, w_in, w_out)
    return out.reshape(bsz, s, d)


def _mlstm_in_kernel(x_ref, n_ref, wm_ref, wt_ref, bg_ref, q_ref, v_ref, o_ref, kt_ref, gt_ref, h_scr, *, sub):
    piece = ROW_PIECE
    n_q = QK_W // FF_CHUNK
    n_v = D_MODEL // FF_CHUNK

    def head(s):
        def norm(r):
            def run():
                rows = slice(s * sub + r * piece, s * sub + (r + 1) * piece)
                h_scr[s % 2, r * piece:(r + 1) * piece, :] = _rms(x_ref[0, rows, :], n_ref[0:1, :]).astype(BF16)
            return run
        return [norm(r) for r in range(sub // piece)]

    def main(s):
        rows = slice(s * sub, (s + 1) * sub)

        def chunk(c):
            def run():
                y = _dot(h_scr[s % 2], wm_ref[:, c * FF_CHUNK:(c + 1) * FF_CHUNK])
                if c < n_q:
                    q_ref[0, rows, c * FF_CHUNK:(c + 1) * FF_CHUNK] = (y * (QK_DIM ** -0.5)).astype(BF16)
                elif c < n_q + n_v:
                    v_ref[0, rows, (c - n_q) * FF_CHUNK:(c - n_q + 1) * FF_CHUNK] = y.astype(BF16)
                else:
                    o_ref[0, rows, (c - n_q - n_v) * FF_CHUNK:(c - n_q - n_v + 1) * FF_CHUNK] = y
            return run

        def transposed():
            tr = lax.dot_general(wt_ref[...], h_scr[s % 2], (((1,), (1,)), ((), ())), preferred_element_type=F32)
            kt_ref[0, :, rows] = tr[:QK_W, :].astype(BF16)
            gt_ref[0, :, rows] = tr[QK_W:, :] + bg_ref[...]

        return [chunk(c) for c in range(n_q + 2 * n_v)] + [transposed]

    _software_pipeline(x_ref.shape[1] // sub, head, main, lambda s: [])


def _mlstm_in(x, norms1, w_main, w_tr, b_gate_col):
    bsz, s, d = x.shape
    tm = min(WIDE_TILE, s)
    sub = min(ROW_GROUP, tm)
    nt = s // tm
    ng = 4 * HEADS
    return pl.pallas_call(
        functools.partial(_mlstm_in_kernel, sub=sub),
        grid=(bsz, nt),
        in_specs=[
            pl.BlockSpec((1, tm, d), lambda b, i: (b, i, 0)),
            _const_spec((1, d)),
            _const_spec((d, QK_W + 2 * d)),
            _const_spec((QK_W + ng, d)),
            _const_spec((ng, 1)),
        ],
        out_specs=[
            pl.BlockSpec((1, tm, QK_W), lambda b, i: (b, i, 0)),
            pl.BlockSpec((1, tm, d), lambda b, i: (b, i, 0)),
            pl.BlockSpec((1, tm, d), lambda b, i: (b, i, 0)),
            pl.BlockSpec((1, QK_W, tm), lambda b, i: (b, 0, i)),
            pl.BlockSpec((1, ng, tm), lambda b, i: (b, 0, i)),
        ],
        out_shape=[
            jax.ShapeDtypeStruct((bsz, s, QK_W), BF16),
            jax.ShapeDtypeStruct((bsz, s, d), BF16),
            jax.ShapeDtypeStruct((bsz, s, d), F32),
            jax.ShapeDtypeStruct((bsz, QK_W, s), BF16),
            jax.ShapeDtypeStruct((bsz, ng, s), F32),
        ],
        scratch_shapes=[pltpu.VMEM((2, sub, d), BF16)],
        compiler_params=_params(2),
        name="mlstm_in_proj",
    )(x, norms1, w_main, w_tr, b_gate_col)


def _log_sigmoid(x):
    return jnp.minimum(x, 0.0) - jnp.log1p(jnp.exp(-jnp.abs(x)))


def _gate_kernel(gt_ref, row_ref, col_ref, a_scr, cm_scr, bc_scr, am_scr, bl_scr, mp_scr, src_scr, *, s):
    nc = s // CHUNK
    hd = HEADS
    g = gt_ref[0]
    pos = lax.broadcasted_iota(jnp.int32, (hd, s), 1) % CHUNK
    prefix, suffix = [], []
    k = 1
    while k < CHUNK:
        prefix.append((k, pos >= k))
        suffix.append((s - k, pos < CHUNK - k))
        k *= 2

    def scan(x, steps, op, fill):
        for amt, ok in steps:
            x = op(x, jnp.where(ok, pltpu.roll(x, amt, 1), fill))
        return x

    for dr in range(2):
        along, against = (prefix, suffix) if dr == 0 else (suffix, prefix)
        end = CHUNK - 1 if dr == 0 else 0
        ig = g[2 * hd * dr:2 * hd * dr + hd, :]
        bc = scan(_log_sigmoid(g[2 * hd * dr + hd:2 * hd * dr + 2 * hd, :]), along, jnp.add, 0.0)
        a = ig - bc
        cm = scan(a, along, jnp.maximum, -jnp.inf)
        rs = slice(hd * dr, hd * dr + hd)
        a_scr[rs, :] = a
        cm_scr[rs, :] = cm
        bc_scr[rs, :] = bc
        am_scr[rs, :] = scan(cm, against, jnp.maximum, -jnp.inf)
        bl_scr[rs, :] = scan(jnp.where(pos == end, bc, 0.0), against, jnp.add, 0.0)

    def chunk_step(c, carry):
        new = []
        for dr in range(2):
            cidx = c if dr == 0 else nc - 1 - c
            cs = pl.ds(pl.multiple_of(cidx * CHUNK, CHUNK), CHUNK)
            rs = slice(hd * dr, hd * dr + hd)
            mp_scr[rs, cs] = carry[dr]
            new.append(bl_scr[rs, cs] + jnp.maximum(carry[dr], am_scr[rs, cs]))
        return tuple(new)

    zero = jnp.zeros((hd, CHUNK), F32)
    lax.fori_loop(0, nc, chunk_step, (zero, zero))

    m_prev = mp_scr[...]
    m_run = jnp.maximum(m_prev, cm_scr[...])
    m_last = jnp.maximum(m_prev, am_scr[...])
    row_ref[0, 0:UNITS, :] = a_scr[...]
    row_ref[0, UNITS:2 * UNITS, :] = jnp.exp(a_scr[...] - m_last)
    row_ref[0, 2 * UNITS:3 * UNITS, :] = jnp.exp(m_prev - m_last)
    w_inter = jnp.exp(m_prev - m_run)
    clamp = jnp.exp(-(bc_scr[...] + m_run))
    for dr in range(2):
        rs = slice(hd * dr, hd * dr + hd)
        base = 3 * hd * dr
        src_scr[base:base + hd, :] = m_run[rs, :]
        src_scr[base + hd:base + 2 * hd, :] = w_inter[rs, :]
        src_scr[base + 2 * hd:base + 3 * hd, :] = clamp[rs, :]
    src_scr[3 * UNITS:, :] = jnp.zeros((CHUNK - 3 * UNITS, s), F32)

    def transpose_step(c, carry):
        cs = pl.ds(pl.multiple_of(c * CHUNK, CHUNK), CHUNK)
        col_ref[0, cs, :] = src_scr[:, cs].T
        return carry

    lax.fori_loop(0, nc, transpose_step, 0, unroll=4)


def _gate_scan(gates_t):
    bsz, ng, s = gates_t.shape
    kern = functools.partial(_gate_kernel, s=s)
    return pl.pallas_call(
        kern,
        grid=(bsz,),
        in_specs=[pl.BlockSpec((1, ng, s), lambda b: (b, 0, 0))],
        out_specs=[
            pl.BlockSpec((1, 3 * UNITS, s), lambda b: (b, 0, 0)),
            pl.BlockSpec((1, s, CHUNK), lambda b: (b, 0, 0)),
        ],
        out_shape=[
            jax.ShapeDtypeStruct((bsz, 3 * UNITS, s), F32),
            jax.ShapeDtypeStruct((bsz, s, CHUNK), F32),
        ],
        scratch_shapes=[
            pltpu.VMEM((UNITS, s), F32),
            pltpu.VMEM((UNITS, s), F32),
            pltpu.VMEM((UNITS, s), F32),
            pltpu.VMEM((UNITS, s), F32),
            pltpu.VMEM((UNITS, s), F32),
            pltpu.VMEM((UNITS, s), F32),
            pltpu.VMEM((CHUNK, s), F32),
        ],
        compiler_params=_params(1),
        name="mlstm_gate_scan",
    )(gates_t)


def _mlstm_chunk(dr, q_ref, kt_ref, v_ref, row_ref, col_ref, h_ref, cn_ref, r0):
    hd = HEADS
    rs = slice(r0, r0 + CHUNK)
    rows = row_ref[0, :, rs]
    cols = col_ref[0, rs, :]
    t_idx = lax.broadcasted_iota(jnp.int32, (CHUNK, CHUNK), 0)
    s_idx = lax.broadcasted_iota(jnp.int32, (CHUNK, CHUNK), 1)
    causal = (s_idx <= t_idx) if dr == 0 else (s_idx >= t_idx)
    ones = jnp.ones((CHUNK, V_DIM), BF16)
    for p in range(hd // 2):
        ps = slice(p * CHUNK, (p + 1) * CHUNK)
        q_pair = q_ref[0, rs, ps]
        kt_pair = kt_ref[0, ps, rs]
        cn_pair = cn_ref[ps, :]
        cn_bf = cn_pair.astype(BF16)
        for hh in range(2):
            h = 2 * p + hh
            mine = (s_idx >= QK_DIM) if hh else (s_idx < QK_DIM)
            q_h = jnp.where(mine, q_pair, jnp.zeros_like(q_pair))
            scores = _dot(q_h, kt_pair)
            u = hd * dr + h
            a_row = rows[u:u + 1, :]
            ws_row = rows[UNITS + u:UNITS + u + 1, :]
            dec_row = rows[2 * UNITS + u:2 * UNITS + u + 1, :]
            cb = 3 * hd * dr + h
            m_col = cols[:, cb:cb + 1]
            wi_col = cols[:, cb + hd:cb + hd + 1]
            cl_col = cols[:, cb + 2 * hd:cb + 2 * hd + 1]
            w_intra = jnp.exp(jnp.where(causal, a_row - m_col, -jnp.inf))
            lhs = jnp.concatenate(
                [(scores * w_intra).astype(BF16), (q_h.astype(F32) * wi_col).astype(BF16)], axis=1)
            v1 = jnp.concatenate([v_ref[0, rs, h * V_DIM:(h + 1) * V_DIM], ones], axis=1)
            out = _dot(lhs, jnp.concatenate([v1, cn_bf], axis=0))
            num = out[:, :V_DIM]
            den = out[:, V_DIM:]
            h_ref[0, rs, h * V_DIM:(h + 1) * V_DIM] = num / jnp.maximum(jnp.abs(den), cl_col)
            hs = slice(h * QK_DIM, (h + 1) * QK_DIM)
            kw = (kt_ref[0, hs, rs].astype(F32) * ws_row).astype(BF16)
            decay = jnp.concatenate([dec_row, dec_row], axis=1)
            cn_ref[hs, :] = cn_pair[hh * QK_DIM:(hh + 1) * QK_DIM, :] * decay + _dot(kw, v1)


def _mlstm_kernel(qf_ref, ktf_ref, vf_ref, rowf_ref, colf_ref, qb_ref, ktb_ref, vb_ref, rowb_ref, colb_ref,
                  hf_ref, hb_ref, cnf_ref, cnb_ref, *, cps):
    @pl.when(pl.program_id(1) == 0)
    def _():
        cnf_ref[...] = jnp.zeros_like(cnf_ref)
        cnb_ref[...] = jnp.zeros_like(cnb_ref)

    for c in range(cps):
        _mlstm_chunk(0, qf_ref, ktf_ref, vf_ref, rowf_ref, colf_ref, hf_ref, cnf_ref, c * CHUNK)
        _mlstm_chunk(1, qb_ref, ktb_ref, vb_ref, rowb_ref, colb_ref, hb_ref, cnb_ref, (cps - 1 - c) * CHUNK)


def _mlstm_scan(q, kt, v, rows, cols):
    bsz, s, d = v.shape
    cps = min(CHUNKS_PER_STEP, s // CHUNK)
    tt = cps * CHUNK
    nb = s // tt
    fwd = lambda b, j: j
    bwd = lambda b, j: nb - 1 - j

    def specs(pos):
        return [
            pl.BlockSpec((1, tt, QK_W), lambda b, j: (b, pos(b, j), 0)),
            pl.BlockSpec((1, QK_W, tt), lambda b, j: (b, 0, pos(b, j))),
            pl.BlockSpec((1, tt, d), lambda b, j: (b, pos(b, j), 0)),
            pl.BlockSpec((1, 3 * UNITS, tt), lambda b, j: (b, 0, pos(b, j))),
            pl.BlockSpec((1, tt, CHUNK), lambda b, j: (b, pos(b, j), 0)),
        ]

    kern = functools.partial(_mlstm_kernel, cps=cps)
    return pl.pallas_call(
        kern,
        grid=(bsz, nb),
        in_specs=specs(fwd) + specs(bwd),
        out_specs=[
            pl.BlockSpec((1, tt, d), lambda b, j: (b, j, 0)),
            pl.BlockSpec((1, tt, d), lambda b, j: (b, nb - 1 - j, 0)),
        ],
        out_shape=[jax.ShapeDtypeStruct((bsz, s, d), F32), jax.ShapeDtypeStruct((bsz, s, d), F32)],
        scratch_shapes=[
            pltpu.VMEM((HEADS * QK_DIM, 2 * V_DIM), F32),
            pltpu.VMEM((HEADS * QK_DIM, 2 * V_DIM), F32),
        ],
        compiler_params=_params(2),
        name="mlstm_scan",
    )(q, kt, v, rows, cols, q, kt, v, rows, cols)


def _mlstm_out_ffn_kernel(x_ref, hf_ref, hb_ref, o_ref, n_ref, ng_ref, wmix_ref, win_ref, wout_ref, y_ref,
                          g_scr, mix_scr, x1_scr, h_scr, act_scr, y_scr, *, sub):
    piece = ROW_PIECE
    n_piece = sub // piece
    norms = n_ref[...]

    def rows_of(s, r):
        return slice(s * sub + r * piece, s * sub + (r + 1) * piece)

    def local(r):
        return slice(r * piece, (r + 1) * piece)

    def gate_piece(s, r):
        def run():
            rows = rows_of(s, r)
            h = hf_ref[rows, :] + hb_ref[rows, :]
            parts = []
            for hd in range(HEADS):
                hh = h[:, hd * V_DIM:(hd + 1) * V_DIM]
                parts.append(hh * lax.rsqrt(jnp.mean(hh * hh, axis=-1, keepdims=True) + EPS))
            hn = jnp.concatenate(parts, axis=1) * ng_ref[...]
            g_scr[s % 2, local(r), :] = (jax.nn.sigmoid(o_ref[rows, :]) * hn).astype(BF16)
        return run

    def mix_chunk(s, c):
        def run():
            cs = slice(c * FF_CHUNK, (c + 1) * FF_CHUNK)
            mix_scr[s % 2, :, cs] = _dot(g_scr[s % 2], wmix_ref[:, cs])
        return run

    def residual_piece(s, r):
        def run():
            x1 = x_ref[rows_of(s, r), :] + _rms(mix_scr[s % 2, local(r), :], norms[0:1, :])
            x1_scr[s % 2, local(r), :] = x1
            h_scr[s % 2, local(r), :] = _rms(x1, norms[1:2, :]).astype(BF16)
        return run

    def head(s):
        return ([gate_piece(s, r) for r in range(n_piece)]
                + [mix_chunk(s, c) for c in range(D_MODEL // FF_CHUNK)]
                + [residual_piece(s, r) for r in range(n_piece)])

    x_of = lambda s, r: x1_scr[s % 2, local(r), :]
    _, main, tail = _ffn_tasks(sub, piece, x_of, y_ref, norms[1:3, :], win_ref, wout_ref, h_scr, act_scr, y_scr)
    _software_pipeline(x_ref.shape[0] // sub, head, main, tail)


def _mlstm_out_ffn(x, hf, hb, o, norms3, norm_g, w_mix, w_in, w_out):
    bsz, s, d = x.shape
    t = bsz * s
    tm = min(TOKEN_TILE, t)
    sub = min(ROW_GROUP, tm)
    tok = pl.BlockSpec((tm, d), lambda i: (i, 0))
    out = pl.pallas_call(
        functools.partial(_mlstm_out_ffn_kernel, sub=sub),
        grid=(t // tm,),
        in_specs=[tok, tok, tok, tok, _const_spec((3, d)), _const_spec((1, d)), _const_spec((d, d)),
                  _const_spec((d, 2 * D_FF)), _const_spec((D_FF, d))],
        out_specs=tok,
        out_shape=jax.ShapeDtypeStruct((t, d), F32),
        scratch_shapes=[pltpu.VMEM((2, sub, d), BF16), pltpu.VMEM((2, sub, d), F32), pltpu.VMEM((2, sub, d), F32)]
        + _ffn_scratch(sub),
        compiler_params=_params(1),
        name="mlstm_out_ffn",
    )(x.reshape(t, d), hf.reshape(t, d), hb.reshape(t, d), o.reshape(t, d), norms3, norm_g, w_mix, w_in, w_out)
    return out.reshape(bsz, s, d)


def _trunk(x, p):
    x = _conv_mixer(x, p["n0_mix"], p["conv_w_in"], p["conv_w"], p["conv_w_out"])
    x = _ffn(x, p["n0_ffn"], p["ffn_w_in0"], p["ffn_w_out0"])
    q, v, o, kt, gt = _mlstm_in(x, p["n1_in"], p["ml_w_main"], p["ml_w_tr"], p["ml_b_gate"])
    rows, cols = _gate_scan(gt)
    hf, hb = _mlstm_scan(q, kt, v, rows, cols)
    return _mlstm_out_ffn(x, hf, hb, o, p["n1_post_ffn"], p["ml_norm"], p["ml_w_out"],
                          p["ffn_w_in1"], p["ffn_w_out1"])


def kernel(x_prompt, x_sample, norms, conv_w_in, conv_w, conv_w_out, mlstm_w_in, mlstm_b_gate, mlstm_norm,
           mlstm_w_out, ffn_w_in, ffn_w_out):
    d = D_MODEL
    w_ml = mlstm_w_in[0]
    w_q, w_k = w_ml[:, :QK_W], w_ml[:, QK_W:2 * QK_W]
    w_vo = w_ml[:, 2 * QK_W:2 * QK_W + 2 * d]
    w_g = w_ml[:, 2 * QK_W + 2 * d:]
    p = {
        "n0_mix": norms[0, 0:2],
        "n0_ffn": norms[0, 2:4],
        "n1_in": norms[1, 0:1],
        "n1_post_ffn": norms[1, 1:4],
        "conv_w_in": conv_w_in[0].astype(BF16),
        "conv_w": conv_w[0],
        "conv_w_out": conv_w_out[0].astype(BF16),
        "ffn_w_in0": ffn_w_in[0].astype(BF16),
        "ffn_w_out0": ffn_w_out[0].astype(BF16),
        "ffn_w_in1": ffn_w_in[1].astype(BF16),
        "ffn_w_out1": ffn_w_out[1].astype(BF16),
        "ml_w_main": jnp.concatenate([w_q, w_vo], axis=1).astype(BF16),
        "ml_w_tr": jnp.concatenate([w_k, w_g], axis=1).T.astype(BF16),
        "ml_b_gate": mlstm_b_gate[0].reshape(4 * HEADS, 1),
        "ml_norm": mlstm_norm[0].reshape(1, d),
        "ml_w_out": mlstm_w_out[0].astype(BF16),
    }
    return (_trunk(x_prompt, p), _trunk(x_sample, p))
```

```python
import functools

import jax
import jax.numpy as jnp
from jax import lax
from jax.experimental import pallas as pl
from jax.experimental.pallas import tpu as pltpu

F32 = jnp.float32
BF16 = jnp.bfloat16

D_MODEL = 1024
HEADS = 8
QK_DIM = 64
V_DIM = 128
QK_W = HEADS * QK_DIM
CHUNK = 128
D_FF = 2816
EPS = 1e-6
LOG2E = 1.4426950408889634
CONV_HALO = 16
UNITS = 2 * HEADS

VMEM_LIMIT_BYTES = 56 * 1024 * 1024

TOKEN_TILE = 512
WIDE_TILE = 1024
ROW_GROUP = 256
ROW_PIECE = 32
FF_CHUNK = 256
CHUNKS_PER_STEP = 8


def _rms(x, g):
    ms = jnp.mean(x * x, axis=-1, keepdims=True)
    return x * lax.rsqrt(ms + EPS) * g


def _dot(a, b):
    return jnp.dot(a, b, preferred_element_type=F32)


def _const_spec(shape, layer=None):
    nd = len(shape)
    if layer is None:
        return pl.BlockSpec(shape, lambda *_: (0,) * nd, pipeline_mode=pl.Buffered(1))
    return pl.BlockSpec((None,) + tuple(shape), lambda *_: (layer,) + (0,) * nd, pipeline_mode=pl.Buffered(1))


def _params(n_axes):
    return pltpu.CompilerParams(
        dimension_semantics=("arbitrary",) * n_axes,
        vmem_limit_bytes=VMEM_LIMIT_BYTES,
    )


def _software_pipeline(n_sub, head_tasks, main_tasks, tail_tasks, ahead_tasks=None):
    def run(tasks):
        for task in tasks:
            task()

    if ahead_tasks is None:
        run(head_tasks(0))
    else:
        pl.when(pl.program_id(0) == 0)(lambda: run(head_tasks(0)))
    for s in range(n_sub):
        side = tail_tasks(s - 1) if s > 0 else []
        side = side + (head_tasks(s + 1) if s + 1 < n_sub else (ahead_tasks or []))
        main = main_tasks(s)
        done = 0
        for i, task in enumerate(main):
            task()
            upto = (i + 1) * len(side) // len(main)
            run(side[done:upto])
            done = upto
    run(tail_tasks(n_sub - 1))


def _conv_kernel(x_ref, xp_ref, xn_ref, n_ref, win_ref, cw_ref, wout_ref, o_ref, h_scr, u_scr, *, tm, nt):
    i = pl.program_id(1)
    d = D_MODEL
    hl = CONV_HALO
    g_in = n_ref[0:1, :]
    x = x_ref[0]
    h_scr[0:hl, :] = _rms(xp_ref[0], g_in).astype(BF16)
    h_scr[hl:hl + tm, :] = _rms(x, g_in).astype(BF16)
    h_scr[hl + tm:, :] = _rms(xn_ref[0], g_in).astype(BF16)
    cv = _dot(h_scr[...], win_ref[:, d:3 * d])
    u_scr[...] = cv[:, :d] * cv[:, d:]

    @pl.when(i == 0)
    def _():
        u_scr[0:hl, :] = jnp.zeros((hl, d), F32)

    @pl.when(i == nt - 1)
    def _():
        u_scr[hl + tm:, :] = jnp.zeros((hl, d), F32)

    conv = (u_scr[hl - 1:hl - 1 + tm, :] * cw_ref[0:1, :]
            + u_scr[hl:hl + tm, :] * cw_ref[1:2, :]
            + u_scr[hl + 1:hl + 1 + tm, :] * cw_ref[2:3, :])
    b = _dot(h_scr[hl:hl + tm, :], win_ref[:, 0:d])
    mix = _dot((b * conv).astype(BF16), wout_ref[...])
    o_ref[0] = x + _rms(mix, n_ref[1:2, :])


def _conv_mixer(x, norms, w_in, conv_w, w_out):
    bsz, s, d = x.shape
    tm = min(WIDE_TILE, s)
    nt = s // tm
    hb = tm // CONV_HALO
    last_hb = s // CONV_HALO - 1
    kern = functools.partial(_conv_kernel, tm=tm, nt=nt)
    return pl.pallas_call(
        kern,
        grid=(bsz, nt),
        in_specs=[
            pl.BlockSpec((1, tm, d), lambda b, i: (b, i, 0)),
            pl.BlockSpec((1, CONV_HALO, d), lambda b, i: (b, jnp.maximum(i * hb - 1, 0), 0)),
            pl.BlockSpec((1, CONV_HALO, d), lambda b, i: (b, jnp.minimum((i + 1) * hb, last_hb), 0)),
            _const_spec((4, d), layer=0),
            _const_spec((d, 3 * d), layer=0),
            _const_spec((3, d), layer=0),
            _const_spec((d, d), layer=0),
        ],
        out_specs=pl.BlockSpec((1, tm, d), lambda b, i: (b, i, 0)),
        out_shape=jax.ShapeDtypeStruct((bsz, s, d), F32),
        scratch_shapes=[
            pltpu.VMEM((tm + 2 * CONV_HALO, d), BF16),
            pltpu.VMEM((tm + 2 * CONV_HALO, d), F32),
        ],
        compiler_params=_params(2),
        name="conv_mixer",
    )(x, x, x, norms, w_in, conv_w, w_out)


def _ffn_tasks(sub, piece, x_of, out_ref, norms2, win_ref, wout_ref, h_scr, act_scr, y_scr):
    n_piece = sub // piece
    n_in = D_FF // FF_CHUNK
    n_out = D_MODEL // FF_CHUNK

    def rows_of(s, r):
        return slice(s * sub + r * piece, s * sub + (r + 1) * piece)

    def head(s):
        def norm(r):
            def run():
                h_scr[s % 2, r * piece:(r + 1) * piece, :] = _rms(x_of(s, r), norms2[0:1, :]).astype(BF16)
            return run
        return [norm(r) for r in range(n_piece)]

    def main(s):
        def up_chunk(c):
            def run():
                cs = slice(c * FF_CHUNK, (c + 1) * FF_CHUNK)
                h = h_scr[s % 2]
                gate = _dot(h, win_ref[:, cs])
                up = _dot(h, win_ref[:, D_FF + c * FF_CHUNK:D_FF + (c + 1) * FF_CHUNK])
                act_scr[s % 2, :, cs] = ((gate * jax.nn.sigmoid(gate)) * up).astype(BF16)
            return run

        def down_chunk(c):
            def run():
                cs = slice(c * FF_CHUNK, (c + 1) * FF_CHUNK)
                y_scr[s % 2, :, cs] = _dot(act_scr[s % 2], wout_ref[:, cs])
            return run
        return [up_chunk(c) for c in range(n_in)] + [down_chunk(c) for c in range(n_out)]

    def tail(s):
        def finish(r):
            def run():
                y = y_scr[s % 2, r * piece:(r + 1) * piece, :]
                out_ref[rows_of(s, r), :] = x_of(s, r) + _rms(y, norms2[1:2, :])
            return run
        return [finish(r) for r in range(n_piece)]

    return head, main, tail


def _ffn_kernel(x_ref, n_ref, win_ref, wout_ref, o_ref, h_scr, act_scr, y_scr, *, sub):
    piece = ROW_PIECE
    x_of = lambda s, r: x_ref[s * sub + r * piece:s * sub + (r + 1) * piece, :]
    head, main, tail = _ffn_tasks(sub, piece, x_of, o_ref, n_ref[2:4, :], win_ref, wout_ref, h_scr, act_scr, y_scr)
    _software_pipeline(x_ref.shape[0] // sub, head, main, tail)


def _ffn_scratch(sub):
    return [pltpu.VMEM((2, sub, D_MODEL), BF16), pltpu.VMEM((2, sub, D_FF), BF16), pltpu.VMEM((2, sub, D_MODEL), F32)]


def _ffn(x, norms, w_in, w_out, layer):
    bsz, s, d = x.shape
    t = bsz * s
    tm = min(WIDE_TILE, t)
    sub = min(ROW_GROUP, tm)
    out = pl.pallas_call(
        functools.partial(_ffn_kernel, sub=sub),
        grid=(t // tm,),
        in_specs=[
            pl.BlockSpec((tm, d), lambda i: (i, 0)),
            _const_spec((4, d), layer=layer),
            _const_spec((d, 2 * D_FF), layer=layer),
            _const_spec((D_FF, d), layer=layer),
        ],
        out_specs=pl.BlockSpec((tm, d), lambda i: (i, 0)),
        out_shape=jax.ShapeDtypeStruct((t, d), F32),
        scratch_shapes=_ffn_scratch(sub),
        compiler_params=_params(1),
        name="swiglu_ffn",
    )(x.reshape(t, d), norms, w_in, w_out)
    return out.reshape(bsz, s, d)


def _mlstm_in_kernel(x_ref, n_ref, wm_ref, wt_ref, bg_ref, q_ref, v_ref, o_ref, kt_ref, gt_ref, h_scr, *, sub):
    piece = ROW_PIECE
    n_q = QK_W // FF_CHUNK
    n_v = D_MODEL // FF_CHUNK

    def head(s):
        def norm(r):
            def run():
                rows = slice(s * sub + r * piece, s * sub + (r + 1) * piece)
                h_scr[s % 2, r * piece:(r + 1) * piece, :] = _rms(x_ref[0, rows, :], n_ref[0:1, :]).astype(BF16)
            return run
        return [norm(r) for r in range(sub // piece)]

    def main(s):
        rows = slice(s * sub, (s + 1) * sub)

        def chunk(c):
            def run():
                y = _dot(h_scr[s % 2], wm_ref[:, c * FF_CHUNK:(c + 1) * FF_CHUNK])
                if c < n_q:
                    q_ref[0, rows, c * FF_CHUNK:(c + 1) * FF_CHUNK] = (y * (QK_DIM ** -0.5)).astype(BF16)
                elif c < n_q + n_v:
                    v_ref[0, rows, (c - n_q) * FF_CHUNK:(c - n_q + 1) * FF_CHUNK] = y.astype(BF16)
                else:
                    o_ref[0, rows, (c - n_q - n_v) * FF_CHUNK:(c - n_q - n_v + 1) * FF_CHUNK] = y.astype(BF16)
            return run

        def transposed():
            tr = lax.dot_general(wt_ref[...], h_scr[s % 2], (((1,), (1,)), ((), ())), preferred_element_type=F32)
            kt_ref[0, :, rows] = tr[:QK_W, :].astype(BF16)
            gt_ref[0, :, rows] = tr[QK_W:, :] + bg_ref[...]

        return [chunk(c) for c in range(n_q + 2 * n_v)] + [transposed]

    _software_pipeline(x_ref.shape[1] // sub, head, main, lambda s: [])


def _mlstm_in(x, norms, w_main, w_tr, b_gate_col):
    bsz, s, d = x.shape
    tm = min(WIDE_TILE, s)
    sub = min(ROW_GROUP, tm)
    nt = s // tm
    ng = 4 * HEADS
    return pl.pallas_call(
        functools.partial(_mlstm_in_kernel, sub=sub),
        grid=(bsz, nt),
        in_specs=[
            pl.BlockSpec((1, tm, d), lambda b, i: (b, i, 0)),
            _const_spec((4, d), layer=1),
            _const_spec((d, QK_W + 2 * d)),
            _const_spec((QK_W + ng, d)),
            _const_spec((ng, 1)),
        ],
        out_specs=[
            pl.BlockSpec((1, tm, QK_W), lambda b, i: (b, i, 0)),
            pl.BlockSpec((1, tm, d), lambda b, i: (b, i, 0)),
            pl.BlockSpec((1, tm, d), lambda b, i: (b, i, 0)),
            pl.BlockSpec((1, QK_W, tm), lambda b, i: (b, 0, i)),
            pl.BlockSpec((1, ng, tm), lambda b, i: (b, 0, i)),
        ],
        out_shape=[
            jax.ShapeDtypeStruct((bsz, s, QK_W), BF16),
            jax.ShapeDtypeStruct((bsz, s, d), BF16),
            jax.ShapeDtypeStruct((bsz, s, d), BF16),
            jax.ShapeDtypeStruct((bsz, QK_W, s), BF16),
            jax.ShapeDtypeStruct((bsz, ng, s), F32),
        ],
        scratch_shapes=[pltpu.VMEM((2, sub, d), BF16)],
        compiler_params=_params(2),
        name="mlstm_in_proj",
    )(x, norms, w_main, w_tr, b_gate_col)


def _log_sigmoid(x):
    return jnp.minimum(x, 0.0) - jnp.log1p(jnp.exp(-jnp.abs(x)))


def _gate_kernel(gt_ref, row_ref, col_ref, a_scr, cm_scr, bc_scr, am_scr, bl_scr, mp_scr, src_scr, *, s):
    nc = s // CHUNK
    hd = HEADS
    g = gt_ref[0]
    pos = lax.broadcasted_iota(jnp.int32, (hd, s), 1) % CHUNK
    prefix, suffix = [], []
    k = 1
    while k < CHUNK:
        prefix.append((k, pos >= k))
        suffix.append((s - k, pos < CHUNK - k))
        k *= 2

    def scan(x, steps, op, fill):
        for amt, ok in steps:
            x = op(x, jnp.where(ok, pltpu.roll(x, amt, 1), fill))
        return x

    for dr in range(2):
        along, against = (prefix, suffix) if dr == 0 else (suffix, prefix)
        end = CHUNK - 1 if dr == 0 else 0
        ig = g[2 * hd * dr:2 * hd * dr + hd, :]
        bc = scan(_log_sigmoid(g[2 * hd * dr + hd:2 * hd * dr + 2 * hd, :]), along, jnp.add, 0.0)
        a = ig - bc
        cm = scan(a, along, jnp.maximum, -jnp.inf)
        rs = slice(hd * dr, hd * dr + hd)
        a_scr[rs, :] = a
        cm_scr[rs, :] = cm
        bc_scr[rs, :] = bc
        am_scr[rs, :] = scan(cm, against, jnp.maximum, -jnp.inf)
        bl_scr[rs, :] = scan(jnp.where(pos == end, bc, 0.0), against, jnp.add, 0.0)

    def chunk_step(c, carry):
        new = []
        for dr in range(2):
            cidx = c if dr == 0 else nc - 1 - c
            cs = pl.ds(pl.multiple_of(cidx * CHUNK, CHUNK), CHUNK)
            rs = slice(hd * dr, hd * dr + hd)
            mp_scr[rs, cs] = carry[dr]
            new.append(bl_scr[rs, cs] + jnp.maximum(carry[dr], am_scr[rs, cs]))
        return tuple(new)

    zero = jnp.zeros((hd, CHUNK), F32)
    lax.fori_loop(0, nc, chunk_step, (zero, zero))

    m_prev = mp_scr[...]
    m_run = jnp.maximum(m_prev, cm_scr[...])
    m_last = jnp.maximum(m_prev, am_scr[...])
    a = a_scr[...]
    row_ref[0, 0:UNITS, :] = a * LOG2E
    row_ref[0, UNITS:2 * UNITS, :] = jnp.exp(a - m_last)
    row_ref[0, 2 * UNITS:3 * UNITS, :] = jnp.exp(m_prev - m_last)
    row_ref[0, 3 * UNITS:4 * UNITS, :] = (m_prev - a) * LOG2E
    src_scr[0:UNITS, :] = m_run * LOG2E
    src_scr[UNITS:2 * UNITS, :] = jnp.exp(-(bc_scr[...] + m_run))
    src_scr[2 * UNITS:, :] = jnp.zeros((CHUNK - 2 * UNITS, s), F32)

    def transpose_step(c, carry):
        cs = pl.ds(pl.multiple_of(c * CHUNK, CHUNK), CHUNK)
        col_ref[0, cs, :] = src_scr[:, cs].T
        return carry

    lax.fori_loop(0, nc, transpose_step, 0, unroll=4)


def _gate_scan(gates_t):
    bsz, ng, s = gates_t.shape
    kern = functools.partial(_gate_kernel, s=s)
    return pl.pallas_call(
        kern,
        grid=(bsz,),
        in_specs=[pl.BlockSpec((1, ng, s), lambda b: (b, 0, 0))],
        out_specs=[
            pl.BlockSpec((1, 4 * UNITS, s), lambda b: (b, 0, 0)),
            pl.BlockSpec((1, s, CHUNK), lambda b: (b, 0, 0)),
        ],
        out_shape=[
            jax.ShapeDtypeStruct((bsz, 4 * UNITS, s), F32),
            jax.ShapeDtypeStruct((bsz, s, CHUNK), F32),
        ],
        scratch_shapes=[
            pltpu.VMEM((UNITS, s), F32),
            pltpu.VMEM((UNITS, s), F32),
            pltpu.VMEM((UNITS, s), F32),
            pltpu.VMEM((UNITS, s), F32),
            pltpu.VMEM((UNITS, s), F32),
            pltpu.VMEM((UNITS, s), F32),
            pltpu.VMEM((CHUNK, s), F32),
        ],
        compiler_params=_params(1),
        name="mlstm_gate_scan",
    )(gates_t)


def _mlstm_chunk(dr, q_ref, kt_ref, v_ref, row_ref, col_ref, h_ref, cn_ref, r0):
    hd = HEADS
    rs = slice(r0, r0 + CHUNK)
    rows = row_ref[0, :, rs]
    cols = col_ref[0, rs, :]
    t_idx = lax.broadcasted_iota(jnp.int32, (CHUNK, CHUNK), 0)
    s_idx = lax.broadcasted_iota(jnp.int32, (CHUNK, CHUNK), 1)
    causal = (s_idx <= t_idx) if dr == 0 else (s_idx >= t_idx)
    ones = jnp.ones((CHUNK, V_DIM), BF16)
    for p in range(hd // 2):
        ps = slice(p * CHUNK, (p + 1) * CHUNK)
        q_pair = q_ref[0, rs, ps]
        kt_pair = kt_ref[0, ps, rs]
        cn_pair = cn_ref[ps, :]
        cn_bf = cn_pair.astype(BF16)
        for hh in range(2):
            h = 2 * p + hh
            mine = (s_idx >= QK_DIM) if hh else (s_idx < QK_DIM)
            q_h = jnp.where(mine, q_pair, jnp.zeros_like(q_pair))
            scores = _dot(q_h, kt_pair)
            u = hd * dr + h
            a_row = rows[u:u + 1, :]
            ws_row = rows[UNITS + u:UNITS + u + 1, :]
            dec_row = rows[2 * UNITS + u:2 * UNITS + u + 1, :]
            mpa_row = rows[3 * UNITS + u:3 * UNITS + u + 1, :]
            m_col = cols[:, u:u + 1]
            cl_col = cols[:, UNITS + u:UNITS + u + 1]
            arg = a_row - m_col
            w_intra = jnp.exp2(jnp.where(causal, arg, -jnp.inf))
            w_inter = jnp.exp2(arg + mpa_row)
            lhs = jnp.concatenate(
                [(scores * w_intra).astype(BF16), (q_h.astype(F32) * w_inter).astype(BF16)], axis=1)
            v1 = jnp.concatenate([v_ref[0, rs, h * V_DIM:(h + 1) * V_DIM], ones], axis=1)
            out = _dot(lhs, jnp.concatenate([v1, cn_bf], axis=0))
            num = out[:, :V_DIM]
            den = out[:, V_DIM:]
            h_ref[0, rs, h * V_DIM:(h + 1) * V_DIM] = num / jnp.maximum(jnp.abs(den), cl_col)
            hs = slice(h * QK_DIM, (h + 1) * QK_DIM)
            kw = (kt_ref[0, hs, rs].astype(F32) * ws_row).astype(BF16)
            decay = jnp.concatenate([dec_row, dec_row], axis=1)
            cn_ref[hs, :] = cn_pair[hh * QK_DIM:(hh + 1) * QK_DIM, :] * decay + _dot(kw, v1)


def _mlstm_kernel(qf_ref, ktf_ref, vf_ref, rowf_ref, colf_ref, qb_ref, ktb_ref, vb_ref, rowb_ref, colb_ref,
                  hf_ref, hb_ref, cnf_ref, cnb_ref, *, cps):
    @pl.when(pl.program_id(1) == 0)
    def _():
        cnf_ref[...] = jnp.zeros_like(cnf_ref)
        cnb_ref[...] = jnp.zeros_like(cnb_ref)

    for c in range(cps):
        _mlstm_chunk(0, qf_ref, ktf_ref, vf_ref, rowf_ref, colf_ref, hf_ref, cnf_ref, c * CHUNK)
        _mlstm_chunk(1, qb_ref, ktb_ref, vb_ref, rowb_ref, colb_ref, hb_ref, cnb_ref, (cps - 1 - c) * CHUNK)


def _mlstm_scan(q, kt, v, rows, cols):
    bsz, s, d = v.shape
    cps = min(CHUNKS_PER_STEP, s // CHUNK)
    tt = cps * CHUNK
    nb = s // tt
    fwd = lambda b, j: j
    bwd = lambda b, j: nb - 1 - j

    def specs(pos):
        return [
            pl.BlockSpec((1, tt, QK_W), lambda b, j: (b, pos(b, j), 0)),
            pl.BlockSpec((1, QK_W, tt), lambda b, j: (b, 0, pos(b, j))),
            pl.BlockSpec((1, tt, d), lambda b, j: (b, pos(b, j), 0)),
            pl.BlockSpec((1, 4 * UNITS, tt), lambda b, j: (b, 0, pos(b, j))),
            pl.BlockSpec((1, tt, CHUNK), lambda b, j: (b, pos(b, j), 0)),
        ]

    kern = functools.partial(_mlstm_kernel, cps=cps)
    return pl.pallas_call(
        kern,
        grid=(bsz, nb),
        in_specs=specs(fwd) + specs(bwd),
        out_specs=[
            pl.BlockSpec((1, tt, d), lambda b, j: (b, j, 0)),
            pl.BlockSpec((1, tt, d), lambda b, j: (b, nb - 1 - j, 0)),
        ],
        out_shape=[jax.ShapeDtypeStruct((bsz, s, d), F32), jax.ShapeDtypeStruct((bsz, s, d), F32)],
        scratch_shapes=[
            pltpu.VMEM((HEADS * QK_DIM, 2 * V_DIM), F32),
            pltpu.VMEM((HEADS * QK_DIM, 2 * V_DIM), F32),
        ],
        compiler_params=_params(2),
        name="mlstm_scan",
    )(q, kt, v, rows, cols, q, kt, v, rows, cols)


def _mlstm_out_ffn_kernel(x_ref, hf_ref, hb_ref, o_ref, xa_ref, hfa_ref, hba_ref, oa_ref, n_ref, ng_ref, wmix_ref,
                          win_ref, wout_ref, y_ref, g_scr, mix_scr, x1_scr, h_scr, act_scr, y_scr, *, sub):
    piece = ROW_PIECE
    n_piece = sub // piece
    n_sub = x_ref.shape[0] // sub
    norms = n_ref[1:4, :]

    def local(r):
        return slice(r * piece, (r + 1) * piece)

    def head_of(srcs, base, slot):
        xs_ref, hfs_ref, hbs_ref, os_ref = srcs

        def gate_piece(r):
            def run():
                rows = slice(base + r * piece, base + (r + 1) * piece)
                h = hfs_ref[rows, :] + hbs_ref[rows, :]
                parts = []
                for hd in range(HEADS):
                    hh = h[:, hd * V_DIM:(hd + 1) * V_DIM]
                    parts.append(hh * lax.rsqrt(jnp.mean(hh * hh, axis=-1, keepdims=True) + EPS))
                hn = jnp.concatenate(parts, axis=1) * ng_ref[...]
                g_scr[slot, local(r), :] = (jax.nn.sigmoid(os_ref[rows, :].astype(F32)) * hn).astype(BF16)
            return run

        def mix_chunk(c):
            def run():
                cs = slice(c * FF_CHUNK, (c + 1) * FF_CHUNK)
                mix_scr[slot, :, cs] = _dot(g_scr[slot], wmix_ref[:, cs])
            return run

        def residual_piece(r):
            def run():
                rows = slice(base + r * piece, base + (r + 1) * piece)
                x1 = xs_ref[rows, :] + _rms(mix_scr[slot, local(r), :], norms[0:1, :])
                x1_scr[slot, local(r), :] = x1
                h_scr[slot, local(r), :] = _rms(x1, norms[1:2, :]).astype(BF16)
            return run

        return ([gate_piece(r) for r in range(n_piece)]
                + [mix_chunk(c) for c in range(D_MODEL // FF_CHUNK)]
                + [residual_piece(r) for r in range(n_piece)])

    here = (x_ref, hf_ref, hb_ref, o_ref)
    head = lambda s: head_of(here, s * sub, s % 2)
    ahead = head_of((xa_ref, hfa_ref, hba_ref, oa_ref), 0, n_sub % 2)
    x_of = lambda s, r: x1_scr[s % 2, local(r), :]
    _, main, tail = _ffn_tasks(sub, piece, x_of, y_ref, norms[1:3, :], win_ref, wout_ref, h_scr, act_scr, y_scr)
    _software_pipeline(n_sub, head, main, tail, ahead_tasks=ahead)


def _mlstm_out_ffn(x, hf, hb, o, norms, norm_g, w_mix, w_in, w_out):
    bsz, s, d = x.shape
    t = bsz * s
    tm = min(TOKEN_TILE, t)
    sub = min(ROW_GROUP, tm)
    assert (tm // sub) % 2 == 0, "scratch slots alternate per row group and must line up across grid steps"
    groups = tm // sub
    last_group = t // sub - groups
    tok = pl.BlockSpec((tm, d), lambda i: (i, 0))
    nxt = pl.BlockSpec((sub, d), lambda i: (jnp.minimum((i + 1) * groups, last_group), 0))
    flat = lambda a: a.reshape(t, d)
    out = pl.pallas_call(
        functools.partial(_mlstm_out_ffn_kernel, sub=sub),
        grid=(t // tm,),
        in_specs=[tok, tok, tok, tok, nxt, nxt, nxt, nxt, _const_spec((4, d), layer=1), _const_spec((1, d)),
                  _const_spec((d, d), layer=0), _const_spec((d, 2 * D_FF), layer=1), _const_spec((D_FF, d), layer=1)],
        out_specs=tok,
        out_shape=jax.ShapeDtypeStruct((t, d), F32),
        scratch_shapes=[pltpu.VMEM((2, sub, d), BF16), pltpu.VMEM((2, sub, d), F32), pltpu.VMEM((2, sub, d), F32)]
        + _ffn_scratch(sub),
        compiler_params=_params(1),
        name="mlstm_out_ffn",
    )(flat(x), flat(hf), flat(hb), flat(o), flat(x), flat(hf), flat(hb), flat(o), norms, norm_g, w_mix, w_in, w_out)
    return out.reshape(bsz, s, d)


def _trunk(x, p):
    x = _conv_mixer(x, p["norms"], p["conv_w_in"], p["conv_w"], p["conv_w_out"])
    x = _ffn(x, p["norms"], p["ffn_w_in"], p["ffn_w_out"], layer=0)
    q, v, o, kt, gt = _mlstm_in(x, p["norms"], p["ml_w_main"], p["ml_w_tr"], p["ml_b_gate"])
    rows, cols = _gate_scan(gt)
    hf, hb = _mlstm_scan(q, kt, v, rows, cols)
    return _mlstm_out_ffn(x, hf, hb, o, p["norms"], p["ml_norm"], p["ml_w_out"], p["ffn_w_in"], p["ffn_w_out"])


def kernel(x_prompt, x_sample, norms, conv_w_in, conv_w, conv_w_out, mlstm_w_in, mlstm_b_gate, mlstm_norm,
           mlstm_w_out, ffn_w_in, ffn_w_out):
    d = D_MODEL
    w_ml = mlstm_w_in[0]
    w_q, w_k = w_ml[:, :QK_W], w_ml[:, QK_W:2 * QK_W]
    w_vo = w_ml[:, 2 * QK_W:2 * QK_W + 2 * d]
    w_g = w_ml[:, 2 * QK_W + 2 * d:]
    p = {
        "norms": norms,
        "conv_w_in": conv_w_in.astype(BF16),
        "conv_w": conv_w,
        "conv_w_out": conv_w_out.astype(BF16),
        "ffn_w_in": ffn_w_in.astype(BF16),
        "ffn_w_out": ffn_w_out.astype(BF16),
        "ml_w_main": jnp.concatenate([w_q, w_vo], axis=1).astype(BF16),
        "ml_w_tr": jnp.concatenate([w_k, w_g], axis=1).T.astype(BF16),
        "ml_b_gate": mlstm_b_gate[0].reshape(4 * HEADS, 1),
        "ml_norm": mlstm_norm,
        "ml_w_out": mlstm_w_out.astype(BF16),
    }
    return (_trunk(x_prompt, p), _trunk(x_sample, p))
```
